```python
import math
import jax, jax.numpy as jnp
from jax import lax
import numpy as np

D_MODEL = 4096
BATCH = 4
SEQ = 4096
DEPTH = 4

CHUNK = 64
Q_BLOCK = 128
N_MIXERS = 4
EPS = 1e-6
ROPE_THETA = 10000.0

D_FF = 5632

FOX_HEADS = 32
FOX_HEAD_DIM = D_MODEL // FOX_HEADS
FOX_IN = 3 * D_MODEL + FOX_HEADS

DIFF_HEADS = 16
DIFF_HEAD_DIM = D_MODEL // (2 * DIFF_HEADS)
DIFF_V_DIM = 2 * DIFF_HEAD_DIM
DIFF_IN = 4 * DIFF_HEADS * DIFF_HEAD_DIM + DIFF_HEADS * DIFF_V_DIM

MLA_HEADS = 32
MLA_Q_RANK = 1024
MLA_KV_RANK = 512
MLA_NOPE_DIM = 128
MLA_ROPE_DIM = 64
MLA_V_DIM = 128
MLA_IN = MLA_Q_RANK + MLA_KV_RANK + MLA_ROPE_DIM

MLSTM_HEADS = 8
MLSTM_QK_DIM = D_MODEL // (2 * MLSTM_HEADS)
MLSTM_V_DIM = D_MODEL // MLSTM_HEADS
MLSTM_IN = 2 * MLSTM_HEADS * MLSTM_QK_DIM + MLSTM_HEADS * MLSTM_V_DIM + D_MODEL + 2 * MLSTM_HEADS

N_OCC = tuple(len(range(m, DEPTH, N_MIXERS)) for m in range(N_MIXERS))

kernel_name = "hybrid_chunk_causal_macaron_trunk"


def rms_norm(x, g):
    xf = x.astype(jnp.float32)
    y = xf * lax.rsqrt(jnp.mean(xf * xf, axis=-1, keepdims=True) + EPS)
    return (y * g.astype(jnp.float32)).astype(x.dtype)


def rope_tables(seq, dim, dtype):
    pos = jnp.arange(seq, dtype=jnp.float32)
    inv = ROPE_THETA ** (-jnp.arange(0, dim, 2, dtype=jnp.float32) / dim)
    ang = pos[:, None] * inv[None, :]
    return jnp.cos(ang).astype(dtype), jnp.sin(ang).astype(dtype)


def apply_rope(t, cos, sin):
    half = t.shape[-1] // 2
    t1, t2 = t[..., :half], t[..., half:]
    return jnp.concatenate([t1 * cos - t2 * sin, t2 * cos + t1 * sin], axis=-1)


def swiglu(h, w_in, w_out):
    g, u = jnp.split(h @ w_in, 2, axis=-1)
    return (jax.nn.silu(g) * u) @ w_out


def blocked_attention(q, k, v, frame_causal, log_decay_cum=None):
    B, H, G, S, dk = q.shape
    dv = v.shape[-1]
    scale = dk ** -0.5
    n_blk = S // Q_BLOCK
    k_pos = jnp.arange(S)

    def one_block(i):
        start = i * Q_BLOCK
        qb = lax.dynamic_slice_in_dim(q, start, Q_BLOCK, axis=3)
        s = jnp.einsum("bhgqd,bhgkd->bhgqk", qb, k, preferred_element_type=jnp.float32) * scale
        q_pos = start + jnp.arange(Q_BLOCK)
        if frame_causal:
            mask = k_pos[None, :] <= q_pos[:, None]
        else:
            mask = (k_pos // CHUNK)[None, :] <= (q_pos // CHUNK)[:, None]
        if log_decay_cum is not None:
            cq = lax.dynamic_slice_in_dim(log_decay_cum, start, Q_BLOCK, axis=2)
            s = s + (cq[..., :, None] - log_decay_cum[..., None, :])[:, :, None]
        p = jax.nn.softmax(jnp.where(mask, s, -jnp.inf), axis=-1)
        return jnp.einsum("bhgqk,bhkv->bhgqv", p.astype(v.dtype), v)

    outs = lax.map(one_block, jnp.arange(n_blk))
    return jnp.moveaxis(outs, 0, 3).reshape(B, H, G, S, dv)


def fox_mixer(h, w_in, q_gain, k_gain, f_bias, w_out):
    B, S, _ = h.shape
    H, dh = FOX_HEADS, FOX_HEAD_DIM
    q, k, v, f = jnp.split(h @ w_in, [H * dh, 2 * H * dh, 3 * H * dh], axis=-1)
    to_heads = lambda t: t.reshape(B, S, H, dh).transpose(0, 2, 1, 3)
    q = rms_norm(to_heads(q), q_gain)[:, :, None]
    k = rms_norm(to_heads(k), k_gain)[:, :, None]
    v = to_heads(v)
    log_f = jax.nn.log_sigmoid((f + f_bias).astype(jnp.float32))
    cum = jnp.cumsum(log_f, axis=1).transpose(0, 2, 1)
    o = blocked_attention(q, k, v, frame_causal=True, log_decay_cum=cum)[:, :, 0]
    return o.transpose(0, 2, 1, 3).reshape(B, S, H * dh) @ w_out


def diff_mixer(h, w_in, q_gain, k_gain, lam_p, sub_gain, w_out, cos, sin, lam_init):
    B, S, _ = h.shape
    H, dh, dv = DIFF_HEADS, DIFF_HEAD_DIM, DIFF_V_DIM
    q, k, v = jnp.split(h @ w_in, [2 * H * dh, 4 * H * dh], axis=-1)
    to_maps = lambda t: t.reshape(B, S, H, 2, dh).transpose(0, 2, 3, 1, 4)
    q = apply_rope(rms_norm(to_maps(q), q_gain), cos, sin)
    k = apply_rope(rms_norm(to_maps(k), k_gain), cos, sin)
    v = v.reshape(B, S, H, dv).transpose(0, 2, 1, 3)
    lp = lam_p.astype(jnp.float32)
    lam = jnp.exp(jnp.sum(lp[0] * lp[1])) - jnp.exp(jnp.sum(lp[2] * lp[3])) + lam_init
    o = blocked_attention(q, k, v, frame_causal=False)
    y = o[:, :, 0].astype(jnp.float32) - lam * o[:, :, 1].astype(jnp.float32)
    y = rms_norm(y, sub_gain) * (1.0 - lam_init)
    return y.astype(h.dtype).transpose(0, 2, 1, 3).reshape(B, S, H * dv) @ w_out


def mla_mixer(h, w_in, q_lat_gain, w_q_up, kv_lat_gain, w_kv_up, q_gain, k_gain, w_out, cos, sin):
    B, S, _ = h.shape
    H, dn, dr, dv = MLA_HEADS, MLA_NOPE_DIM, MLA_ROPE_DIM, MLA_V_DIM
    cq, ckv, k_pe = jnp.split(h @ w_in, [MLA_Q_RANK, MLA_Q_RANK + MLA_KV_RANK], axis=-1)
    q = (rms_norm(cq, q_lat_gain) @ w_q_up).reshape(B, S, H, dn + dr).transpose(0, 2, 1, 3)
    kv = (rms_norm(ckv, kv_lat_gain) @ w_kv_up).reshape(B, S, H, dn + dv).transpose(0, 2, 1, 3)
    k_nope, v = kv[..., :dn], kv[..., dn:]
    k_pe = jnp.broadcast_to(k_pe[:, None], (B, H, S, dr))
    k = jnp.concatenate([k_nope, k_pe], axis=-1)
    q = rms_norm(q, q_gain)
    k = rms_norm(k, k_gain)
    rot = lambda t: jnp.concatenate([t[..., :dn], apply_rope(t[..., dn:], cos, sin)], axis=-1)
    q, k = rot(q)[:, :, None], rot(k)[:, :, None]
    o = blocked_attention(q, k, v, frame_causal=False)[:, :, 0]
    return o.transpose(0, 2, 1, 3).reshape(B, S, H * dv) @ w_out


def mlstm_chunk_step(carry, xs):
    C, n, m = carry
    qc, kc, vc, ic, fc = xs
    L = qc.shape[2]
    causal = jnp.tril(jnp.ones((L, L), dtype=bool))
    b = jnp.cumsum(fc, axis=-1)
    b_last = b[..., -1]
    d = jnp.where(causal, b[..., :, None] - b[..., None, :] + ic[..., None, :], -jnp.inf)
    inter = b + m[..., None]
    m_row = jnp.maximum(jnp.max(d, axis=-1), inter)
    w_intra = jnp.exp(d - m_row[..., None])
    w_inter = jnp.exp(inter - m_row)
    a = w_intra * jnp.einsum("bhld,bhsd->bhls", qc, kc)
    num = jnp.einsum("bhls,bhsv->bhlv", a, vc) + w_inter[..., None] * jnp.einsum("bhld,bhdv->bhlv", qc, C)
    den = jnp.sum(a, axis=-1) + w_inter * jnp.einsum("bhld,bhd->bhl", qc, n)
    hc = num / jnp.maximum(jnp.abs(den), jnp.exp(-m_row))[..., None]
    g = b_last[..., None] - b + ic
    m_new = jnp.maximum(b_last + m, jnp.max(g, axis=-1))
    decay = jnp.exp(b_last + m - m_new)
    wk = jnp.exp(g - m_new[..., None])
    C_new = decay[..., None, None] * C + jnp.einsum("bhl,bhld,bhlv->bhdv", wk, kc, vc)
    n_new = decay[..., None] * n + jnp.einsum("bhl,bhld->bhd", wk, kc)
    return (C_new, n_new, m_new), hc


def mlstm_mixer(h, w_in, gate_bias, out_gain, w_out):
    B, S, _ = h.shape
    H, dk, dv, L = MLSTM_HEADS, MLSTM_QK_DIM, MLSTM_V_DIM, CHUNK
    NC = S // L
    cuts = [H * dk, 2 * H * dk, 2 * H * dk + H * dv, 2 * H * dk + H * dv + D_MODEL,
            2 * H * dk + H * dv + D_MODEL + H]
    q, k, v, o, ig, fg = jnp.split(h @ w_in, cuts, axis=-1)
    i_pre = (ig + gate_bias[0]).astype(jnp.float32)
    log_f = jax.nn.log_sigmoid((fg + gate_bias[1]).astype(jnp.float32))
    to_chunks = lambda t, dd: t.reshape(B, NC, L, H, dd).transpose(1, 0, 3, 2, 4).astype(jnp.float32)
    gate_chunks = lambda t: t.reshape(B, NC, L, H).transpose(1, 0, 3, 2)
    xs = (to_chunks(q, dk) * (dk ** -0.5), to_chunks(k, dk), to_chunks(v, dv),
          gate_chunks(i_pre), gate_chunks(log_f))
    init = (jnp.zeros((B, H, dk, dv), jnp.float32), jnp.zeros((B, H, dk), jnp.float32),
            jnp.zeros((B, H), jnp.float32))
    _, hs = lax.scan(mlstm_chunk_step, init, xs)
    hs = hs.transpose(1, 0, 3, 2, 4).reshape(B, S, H, dv)
    hs = rms_norm(hs, out_gain) * jax.nn.sigmoid(o.astype(jnp.float32)).reshape(B, S, H, dv)
    return hs.astype(h.dtype).reshape(B, S, H * dv) @ w_out


def setup_inputs(seed: int = 0) -> dict:
    key = jax.random.key(seed)
    keys = jax.random.split(key, 40)
    counter = [0]

    def nk():
        kk = keys[counter[0]]
        counter[0] += 1
        return kk

    def dense(shape, fan_in):
        return jax.random.normal(nk(), shape, jnp.float32) * (fan_in ** -0.5)

    def gain(shape):
        return 1.0 + 0.05 * jax.random.normal(nk(), shape, jnp.float32)

    def small(shape, scale):
        return scale * jax.random.normal(nk(), shape, jnp.float32)

    nA, nB, nC, nD = N_OCC
    D = D_MODEL
    return {
        "x": jax.random.normal(nk(), (BATCH, SEQ, D), jnp.float32),
        "norm_ffn": gain((DEPTH, 2, D)),
        "ffn_w_in": dense((DEPTH, 2, D, 2 * D_FF), D),
        "ffn_w_out": dense((DEPTH, 2, D_FF, D), D_FF),
        "norm_mix": gain((DEPTH, D)),
        "fox_w_in": dense((nA, D, FOX_IN), D),
        "fox_q_gain": gain((nA, FOX_HEAD_DIM)),
        "fox_k_gain": gain((nA, FOX_HEAD_DIM)),
        "fox_f_bias": 3.0 + small((nA, FOX_HEADS), 0.5),
        "fox_w_out": dense((nA, D, D), D),
        "diff_w_in": dense((nB, D, DIFF_IN), D),
        "diff_q_gain": gain((nB, DIFF_HEAD_DIM)),
        "diff_k_gain": gain((nB, DIFF_HEAD_DIM)),
        "diff_lambda": small((nB, 4, DIFF_HEAD_DIM), 0.1),
        "diff_sub_gain": gain((nB, DIFF_V_DIM)),
        "diff_w_out": dense((nB, DIFF_HEADS * DIFF_V_DIM, D), DIFF_HEADS * DIFF_V_DIM),
        "mla_w_in": dense((nC, D, MLA_IN), D),
        "mla_q_lat_gain": gain((nC, MLA_Q_RANK)),
        "mla_w_q_up": dense((nC, MLA_Q_RANK, MLA_HEADS * (MLA_NOPE_DIM + MLA_ROPE_DIM)), MLA_Q_RANK),
        "mla_kv_lat_gain": gain((nC, MLA_KV_RANK)),
        "mla_w_kv_up": dense((nC, MLA_KV_RANK, MLA_HEADS * (MLA_NOPE_DIM + MLA_V_DIM)), MLA_KV_RANK),
        "mla_q_gain": gain((nC, MLA_NOPE_DIM + MLA_ROPE_DIM)),
        "mla_k_gain": gain((nC, MLA_NOPE_DIM + MLA_ROPE_DIM)),
        "mla_w_out": dense((nC, MLA_HEADS * MLA_V_DIM, D), MLA_HEADS * MLA_V_DIM),
        "mlstm_w_in": dense((nD, D, MLSTM_IN), D),
        "mlstm_gate_bias": jnp.stack([small((nD, MLSTM_HEADS), 0.1),
                                      3.0 + small((nD, MLSTM_HEADS), 0.5)], axis=1),
        "mlstm_out_gain": gain((nD, MLSTM_HEADS, MLSTM_V_DIM)),
        "mlstm_w_out": dense((nD, MLSTM_HEADS * MLSTM_V_DIM, D), MLSTM_HEADS * MLSTM_V_DIM),
    }


def reference(x, norm_ffn, ffn_w_in, ffn_w_out, norm_mix,
              fox_w_in, fox_q_gain, fox_k_gain, fox_f_bias, fox_w_out,
              diff_w_in, diff_q_gain, diff_k_gain, diff_lambda, diff_sub_gain, diff_w_out,
              mla_w_in, mla_q_lat_gain, mla_w_q_up, mla_kv_lat_gain, mla_w_kv_up,
              mla_q_gain, mla_k_gain, mla_w_out,
              mlstm_w_in, mlstm_gate_bias, mlstm_out_gain, mlstm_w_out):
    S = x.shape[1]
    cos_d, sin_d = rope_tables(S, DIFF_HEAD_DIM, x.dtype)
    cos_m, sin_m = rope_tables(S, MLA_ROPE_DIM, x.dtype)
    for i in range(DEPTH):
        kind, occ = i % N_MIXERS, i // N_MIXERS
        x = x + 0.5 * swiglu(rms_norm(x, norm_ffn[i, 0]), ffn_w_in[i, 0], ffn_w_out[i, 0])
        h = rms_norm(x, norm_mix[i])
        if kind == 0:
            y = fox_mixer(h, fox_w_in[occ], fox_q_gain[occ], fox_k_gain[occ], fox_f_bias[occ], fox_w_out[occ])
        elif kind == 1:
            lam_init = 0.8 - 0.6 * math.exp(-0.3 * i)
            y = diff_mixer(h, diff_w_in[occ], diff_q_gain[occ], diff_k_gain[occ], diff_lambda[occ],
                           diff_sub_gain[occ], diff_w_out[occ], cos_d, sin_d, lam_init)
        elif kind == 2:
            y = mla_mixer(h, mla_w_in[occ], mla_q_lat_gain[occ], mla_w_q_up[occ], mla_kv_lat_gain[occ],
                          mla_w_kv_up[occ], mla_q_gain[occ], mla_k_gain[occ], mla_w_out[occ], cos_m, sin_m)
        else:
            y = mlstm_mixer(h, mlstm_w_in[occ], mlstm_gate_bias[occ], mlstm_out_gain[occ], mlstm_w_out[occ])
        x = x + y
        x = x + 0.5 * swiglu(rms_norm(x, norm_ffn[i, 1]), ffn_w_in[i, 1], ffn_w_out[i, 1])
    return x
```

```python
import functools
import math

import jax
import jax.numpy as jnp
from jax import lax
from jax.experimental import pallas as pl
from jax.experimental.pallas import tpu as pltpu

F32 = jnp.float32
BF16 = jnp.bfloat16

CHUNK = 64
EPS = 1e-6
ROPE_THETA = 10000.0
N_MIXERS = 4
FOX_HEADS = 32
DIFF_HEADS = 16
MLA_HEADS = 32
MLA_NOPE_DIM = 128
MLA_ROPE_DIM = 64
MLA_V_DIM = 128
MLSTM_HEADS = 8

V7X_LANES = 128
V7X_VMEM_LIMIT_BYTES = 56 * 1024 * 1024

MM_TM = 1024
MM_TN = 1024
ATTN_TQ = 512
MLSTM_LC = 256
NORM_TR = 256
SCAN_BLK = 512


def _tile(n, pref):
    t = min(n, pref)
    assert n % t == 0, (n, pref)
    return t


def _params(n_grid):
    return pltpu.CompilerParams(
        dimension_semantics=("arbitrary",) * n_grid,
        vmem_limit_bytes=V7X_VMEM_LIMIT_BYTES,
    )


def _rmsnorm_body(x_ref, g_ref, o_ref):
    x = x_ref[...]
    ms = jnp.mean(x * x, axis=-1, keepdims=True)
    o_ref[...] = (x * lax.rsqrt(ms + EPS) * g_ref[...]).astype(o_ref.dtype)


def rmsnorm(x, g, name):
    t, d = x.shape
    tr = _tile(t, NORM_TR)
    return pl.pallas_call(
        _rmsnorm_body,
        out_shape=jax.ShapeDtypeStruct((t, d), BF16),
        grid=(t // tr,),
        in_specs=[pl.BlockSpec((tr, d), lambda i: (i, 0)),
                  pl.BlockSpec((1, d), lambda i: (0, 0))],
        out_specs=pl.BlockSpec((tr, d), lambda i: (i, 0)),
        compiler_params=_params(1),
        name=name,
    )(x, g.reshape(1, d).astype(F32))


def _mm_body(*refs, n_w, n_aux, epilogue):
    a_ref = refs[0]
    w_refs = refs[1:1 + n_w]
    aux_refs = refs[1 + n_w:1 + n_w + n_aux]
    out_refs = refs[1 + n_w + n_aux:]
    a = a_ref[...]
    accs = [jnp.dot(a, w[...], preferred_element_type=F32) for w in w_refs]
    epilogue(accs, aux_refs, out_refs)


def matmul(a, ws, w_blk_offsets, n_cols, *, tm, tn, epilogue, out_shapes, out_specs,
           aux=(), aux_specs=(), name):
    t, k = a.shape
    tm = _tile(t, tm)
    tn = _tile(n_cols, tn)
    in_specs = [pl.BlockSpec((tm, k), lambda i, j: (i, 0))]
    for off in w_blk_offsets:
        in_specs.append(pl.BlockSpec((k, tn), lambda i, j, off=off: (0, j + off)))
    in_specs += list(aux_specs)
    body = functools.partial(_mm_body, n_w=len(ws), n_aux=len(aux), epilogue=epilogue)
    return pl.pallas_call(
        body,
        out_shape=out_shapes,
        grid=(t // tm, n_cols // tn),
        in_specs=in_specs,
        out_specs=out_specs,
        compiler_params=_params(2),
        name=name,
    )(a, *ws, *aux)


def _ep_plain(accs, aux, outs):
    outs[0][...] = accs[0].astype(outs[0].dtype)


def _ep_swiglu(accs, aux, outs):
    g, u = accs
    outs[0][...] = (g * jax.nn.sigmoid(g) * u).astype(outs[0].dtype)


def _ep_residual(accs, aux, outs, *, alpha):
    outs[0][...] = aux[0][...] + alpha * accs[0]


def _rope(t, cosf, sinf):
    return t * cosf + pltpu.roll(t, 64, 1) * sinf


def _ep_headnorm(accs, aux, outs, *, rope):
    acc = accs[0]
    gain = aux[0][...]
    if rope:
        cosf, sinf = aux[1][...], aux[2][...]
    for c in range(acc.shape[1] // V7X_LANES):
        sl = slice(c * V7X_LANES, (c + 1) * V7X_LANES)
        t = acc[:, sl]
        y = t * lax.rsqrt(jnp.mean(t * t, axis=-1, keepdims=True) + EPS) * gain[:, sl]
        if rope:
            y = _rope(y, cosf, sinf)
        outs[0][:, sl] = y.astype(outs[0].dtype)


def _ep_rownorm(accs, aux, outs):
    t = accs[0]
    y = t * lax.rsqrt(jnp.mean(t * t, axis=-1, keepdims=True) + EPS) * aux[0][...]
    outs[0][...] = y.astype(outs[0].dtype)


def _ep_mla_q(accs, aux, outs, *, n_valid):
    acc = accs[0]
    gain, cosf, sinf = aux[0][...], aux[1][...], aux[2][...]
    w = 2 * V7X_LANES
    for c in range(acc.shape[1] // w):
        t = acc[:, c * w:(c + 1) * w]
        r = lax.rsqrt(jnp.sum(t * t, axis=-1, keepdims=True) / n_valid + EPS)
        y = t * r * gain
        outs[0][:, c * w:c * w + V7X_LANES] = y[:, :V7X_LANES].astype(outs[0].dtype)
        outs[0][:, c * w + V7X_LANES:(c + 1) * w] = _rope(y[:, V7X_LANES:], cosf, sinf).astype(outs[0].dtype)


def _ep_mla_kv(accs, aux, outs, *, n_valid):
    acc = accs[0]
    kpe, gain_nope, gain_pe, cosf, sinf = (r[...] for r in aux)
    k_out, v_out = outs
    ssq_pe = jnp.sum(kpe * kpe, axis=-1, keepdims=True)
    pe_rot = _rope(kpe * gain_pe, cosf, sinf)
    w = 2 * V7X_LANES
    for c in range(acc.shape[1] // w):
        kn = acc[:, c * w:c * w + V7X_LANES]
        v = acc[:, c * w + V7X_LANES:(c + 1) * w]
        r = lax.rsqrt((jnp.sum(kn * kn, axis=-1, keepdims=True) + ssq_pe) / n_valid + EPS)
        k_out[:, c * w:c * w + V7X_LANES] = (kn * r * gain_nope).astype(k_out.dtype)
        k_out[:, c * w + V7X_LANES:(c + 1) * w] = (pe_rot * r).astype(k_out.dtype)
        v_out[:, c * V7X_LANES:(c + 1) * V7X_LANES] = v.astype(v_out.dtype)


def _out_tile(t, n, dtype, tm, tn):
    tm, tn = _tile(t, tm), _tile(n, tn)
    return jax.ShapeDtypeStruct((t, n), dtype), pl.BlockSpec((tm, tn), lambda i, j: (i, j))


def mm_plain(a, w, out_dtype, name, tm=MM_TM, tn=MM_TN):
    t, n = a.shape[0], w.shape[1]
    shape, spec = _out_tile(t, n, out_dtype, tm, tn)
    return matmul(a, [w], [0], n, tm=tm, tn=tn, epilogue=_ep_plain,
                  out_shapes=shape, out_specs=spec, name=name)


def mm_swiglu(a, w_in, name, tm=MM_TM, tn=MM_TN // 2):
    t, n = a.shape[0], w_in.shape[1] // 2
    tn = _tile(n, tn)
    shape, spec = _out_tile(t, n, BF16, tm, tn)
    return matmul(a, [w_in, w_in], [0, n // tn], n, tm=tm, tn=tn, epilogue=_ep_swiglu,
                  out_shapes=shape, out_specs=spec, name=name)


def mm_residual(a, w, res, alpha, name, tm=MM_TM, tn=MM_TN // 2):
    t, n = a.shape[0], w.shape[1]
    shape, spec = _out_tile(t, n, F32, tm, tn)
    return matmul(a, [w], [0], n, tm=tm, tn=tn,
                  epilogue=functools.partial(_ep_residual, alpha=alpha),
                  out_shapes=shape, out_specs=spec, aux=[res], aux_specs=[spec], name=name)


def _rope_specs(t, seq, tm):
    tm = _tile(t, tm)
    assert seq % tm == 0
    nblk = seq // tm
    return pl.BlockSpec((tm, V7X_LANES), lambda i, j: (i % nblk, 0))


def mm_headnorm(a, w, gain_row, name, rope_tabs=None, seq=None, tm=MM_TM, tn=MM_TN):
    t, n = a.shape[0], w.shape[1]
    tn_ = _tile(n, tn)
    shape, spec = _out_tile(t, n, BF16, tm, tn)
    aux = [gain_row]
    aux_specs = [pl.BlockSpec((1, tn_), lambda i, j: (0, j))]
    if rope_tabs is not None:
        rs = _rope_specs(t, seq, tm)
        aux += list(rope_tabs)
        aux_specs += [rs, rs]
    return matmul(a, [w], [0], n, tm=tm, tn=tn,
                  epilogue=functools.partial(_ep_headnorm, rope=rope_tabs is not None),
                  out_shapes=shape, out_specs=spec, aux=aux, aux_specs=aux_specs, name=name)


def mm_rownorm(a, w, gain_row, name, tm=MM_TM):
    t, n = a.shape[0], w.shape[1]
    shape, spec = _out_tile(t, n, BF16, tm, n)
    return matmul(a, [w], [0], n, tm=tm, tn=n, epilogue=_ep_rownorm,
                  out_shapes=shape, out_specs=spec, aux=[gain_row],
                  aux_specs=[pl.BlockSpec((1, n), lambda i, j: (0, 0))], name=name)


def _gate_scan_body(x_ref, bias_ref, mask_ref, o_ref, carry_ref, *, blk):
    @pl.when(pl.program_id(1) == 0)
    def _():
        carry_ref[...] = jnp.zeros_like(carry_ref)

    y = x_ref[...] + bias_ref[...]
    c = jnp.minimum(y, 0.0) - jnp.log1p(jnp.exp(-jnp.abs(y)))
    row = lax.broadcasted_iota(jnp.int32, c.shape, 0)
    shift = 1
    while shift < blk:
        c = c + jnp.where(row >= shift, pltpu.roll(c, shift, 0), 0.0)
        shift *= 2
    c = c + carry_ref[...]
    carry_ref[...] = c[blk - 1:blk, :]
    o_ref[...] = jnp.where(mask_ref[...] > 0.0, c, y)


def gate_scan(x, bias_row, mask_row, batch, seq, name):
    blk = _tile(seq, SCAN_BLK)
    nblk = seq // blk
    spec = pl.BlockSpec((blk, V7X_LANES), lambda b, s: (b * nblk + s, 0))
    row = pl.BlockSpec((1, V7X_LANES), lambda b, s: (0, 0))
    return pl.pallas_call(
        functools.partial(_gate_scan_body, blk=blk),
        out_shape=jax.ShapeDtypeStruct(x.shape, F32),
        grid=(batch, nblk),
        in_specs=[spec, row, row],
        out_specs=spec,
        scratch_shapes=[pltpu.VMEM((1, V7X_LANES), F32)],
        compiler_params=_params(2),
        name=name,
    )(x, bias_row, mask_row)


def _pad_lanes(v, width=V7X_LANES):
    return jnp.pad(v, [(0, 0)] * (v.ndim - 1) + [(0, width - v.shape[-1])])


def _flash_body(*refs, n_maps, dk, dv, tq, frame_causal, decay, scale, n_extra, finalize):
    q_ref, k_ref, v_ref = refs[:3]
    pos = 3
    if decay:
        cq_ref, ck_ref = refs[3:5]
        pos = 5
    extra = refs[pos:pos + n_extra]
    o_ref = refs[pos + n_extra]
    qi = pl.program_id(2)

    row = lax.broadcasted_iota(jnp.int32, (tq, tq), 0)
    col = lax.broadcasted_iota(jnp.int32, (tq, tq), 1)
    if frame_causal:
        diag_mask = col <= row
    else:
        diag_mask = (col // CHUNK) <= (row // CHUNK)

    outs = []
    for g in range(n_maps):
        q = q_ref[:, g * dk:(g + 1) * dk]

        def step(j, carry, masked, g=g, q=q):
            m, l, acc = carry
            start = pl.multiple_of(j * tq, tq)
            k = k_ref[pl.ds(start, tq), g * dk:(g + 1) * dk]
            v = v_ref[pl.ds(start, tq), :]
            s = lax.dot_general(q, k, (((1,), (1,)), ((), ())),
                                preferred_element_type=F32) * scale
            if decay:
                s = s + (cq_ref[...] - ck_ref[:, pl.ds(start, tq)])
            if masked:
                s = jnp.where(diag_mask, s, -jnp.inf)
            m_new = jnp.maximum(m, jnp.max(s, axis=-1, keepdims=True))
            alpha = jnp.exp(m - m_new)
            p = jnp.exp(s - m_new)
            l = alpha * l + jnp.sum(p, axis=-1, keepdims=True)
            acc = alpha * acc + jnp.dot(p.astype(BF16), v, preferred_element_type=F32)
            return m_new, l, acc

        init = (jnp.full((tq, 1), -jnp.inf, F32), jnp.zeros((tq, 1), F32),
                jnp.zeros((tq, dv), F32))
        carry = lax.fori_loop(0, qi, functools.partial(step, masked=False), init)
        _, l, acc = step(qi, carry, True)
        outs.append(acc / l)
    finalize(outs, extra, o_ref)


def _fin_single(outs, extra, o_ref):
    o_ref[...] = outs[0].astype(o_ref.dtype)


def _fin_diff(outs, extra, o_ref, *, lam_init):
    lp = extra[0][...]
    sub_gain = extra[1][...]
    lam = (jnp.exp(jnp.sum(lp[0:1] * lp[1:2], axis=-1, keepdims=True))
           - jnp.exp(jnp.sum(lp[2:3] * lp[3:4], axis=-1, keepdims=True)) + lam_init)
    y = outs[0] - lam * outs[1]
    y = y * lax.rsqrt(jnp.mean(y * y, axis=-1, keepdims=True) + EPS) * sub_gain
    o_ref[...] = (y * (1.0 - lam_init)).astype(o_ref.dtype)


def flash_attention(q_arr, k_arr, v_arr, *, batch, seq, heads, n_maps, dk, dv,
                    q_blk_off, k_blk_off, v_blk_off, frame_causal, scale,
                    decay_cols=None, decay_rows=None, extra=(), extra_specs=(),
                    finalize=_fin_single, name):
    tq = _tile(seq, ATTN_TQ)
    assert tq % CHUNK == 0
    nq = seq // tq
    qw = n_maps * dk
    in_specs = [
        pl.BlockSpec((tq, qw), lambda b, h, i: (b * nq + i, h + q_blk_off)),
        pl.BlockSpec((seq, qw), lambda b, h, i: (b, h + k_blk_off)),
        pl.BlockSpec((seq, dv), lambda b, h, i: (b, h + v_blk_off)),
    ]
    args = [q_arr, k_arr, v_arr]
    decay = decay_cols is not None
    if decay:
        in_specs += [pl.BlockSpec((None, None, tq, 1), lambda b, h, i: (b, h, i, 0)),
                     pl.BlockSpec((None, None, 1, seq), lambda b, h, i: (b, h, 0, 0))]
        args += [decay_cols, decay_rows]
    in_specs += list(extra_specs)
    args += list(extra)
    body = functools.partial(_flash_body, n_maps=n_maps, dk=dk, dv=dv, tq=tq,
                             frame_causal=frame_causal, decay=decay, scale=scale,
                             n_extra=len(extra), finalize=finalize)
    return pl.pallas_call(
        body,
        out_shape=jax.ShapeDtypeStruct((batch * seq, heads * dv), BF16),
        grid=(batch, heads, nq),
        in_specs=in_specs,
        out_specs=pl.BlockSpec((tq, dv), lambda b, h, i: (b * nq + i, h)),
        compiler_params=_params(3),
        name=name,
    )(*args)


def _mlstm_body(q_ref, k_ref, v_ref, o_ref, ipc_ref, cumc_ref, gr_ref, gain_ref, y_ref,
                c_ref, n_ref, m_ref, cprev_ref, *, lc, heads, dk):
    h = pl.program_id(1)

    @pl.when(pl.program_id(2) == 0)
    def _():
        c_ref[...] = jnp.zeros_like(c_ref)
        n_ref[...] = jnp.zeros_like(n_ref)
        m_ref[...] = jnp.zeros_like(m_ref)
        cprev_ref[...] = jnp.zeros_like(cprev_ref)

    qscale = dk ** -0.5
    q = q_ref[...]
    k = k_ref[...]
    v = v_ref[...]
    ip_col = ipc_ref[...]
    ip_row = gr_ref[pl.ds(h, 1), :]
    cum_row = gr_ref[pl.ds(heads + h, 1), :]
    cprev = cprev_ref[...]
    m_prev = m_ref[...]
    b_col = cumc_ref[...] - cprev
    b_row = cum_row - cprev
    b_last = b_row[:, lc - 1:lc]

    row = lax.broadcasted_iota(jnp.int32, (lc, lc), 0)
    col = lax.broadcasted_iota(jnp.int32, (lc, lc), 1)
    d = jnp.where(col <= row, b_col - b_row + ip_row, -jnp.inf)
    inter = b_col + m_prev
    m_row = jnp.maximum(jnp.max(d, axis=-1, keepdims=True), inter)
    w_intra = jnp.exp(d - m_row)
    w_inter = jnp.exp(inter - m_row)

    qk = lax.dot_general(q, k, (((1,), (1,)), ((), ())), preferred_element_type=F32) * qscale
    a = w_intra * qk
    c_old = c_ref[...]
    num = (jnp.dot(a.astype(BF16), v, preferred_element_type=F32)
           + w_inter * (jnp.dot(q, c_old.astype(BF16), preferred_element_type=F32) * qscale))
    qn = jnp.sum(q.astype(F32) * n_ref[...], axis=-1, keepdims=True) * qscale
    den = jnp.sum(a, axis=-1, keepdims=True) + w_inter * qn
    hc = num / jnp.maximum(jnp.abs(den), jnp.exp(-m_row))

    g_row = b_last - b_row + ip_row
    g_col = b_last - b_col + ip_col
    m_new = jnp.maximum(b_last + m_prev, jnp.max(g_row, axis=-1, keepdims=True))
    decay = jnp.exp(b_last + m_prev - m_new)
    kw = k.astype(F32) * jnp.exp(g_col - m_new)
    c_ref[...] = decay * c_old + jnp.dot(kw.T.astype(BF16), v, preferred_element_type=F32)
    n_ref[...] = decay * n_ref[...] + jnp.sum(kw, axis=0, keepdims=True)
    m_ref[...] = m_new
    cprev_ref[...] = cum_row[:, lc - 1:lc]

    y = hc * lax.rsqrt(jnp.mean(hc * hc, axis=-1, keepdims=True) + EPS) * gain_ref[...]
    y_ref[...] = (y * jax.nn.sigmoid(o_ref[...])).astype(y_ref.dtype)


def mlstm_chunks(qkv, o, gate_cols, gate_rows, out_gain, *, batch, seq, heads, dk, dv, name):
    lc = _tile(seq, MLSTM_LC)
    nc = seq // lc
    kb, vb = heads, (2 * heads * dk) // dv
    col_spec = lambda off: pl.BlockSpec((None, None, lc, 1), lambda b, h, c: (b, h + off, c, 0))
    in_specs = [
        pl.BlockSpec((lc, dk), lambda b, h, c: (b * nc + c, h)),
        pl.BlockSpec((lc, dk), lambda b, h, c: (b * nc + c, kb + h)),
        pl.BlockSpec((lc, dv), lambda b, h, c: (b * nc + c, vb + h)),
        pl.BlockSpec((lc, dv), lambda b, h, c: (b * nc + c, h)),
        col_spec(0),
        col_spec(heads),
        pl.BlockSpec((None, 2 * heads, lc), lambda b, h, c: (b, 0, c)),
        pl.BlockSpec((None, 1, dv), lambda b, h, c: (h, 0, 0)),
    ]
    return pl.pallas_call(
        functools.partial(_mlstm_body, lc=lc, heads=heads, dk=dk),
        out_shape=jax.ShapeDtypeStruct((batch * seq, heads * dv), BF16),
        grid=(batch, heads, nc),
        in_specs=in_specs,
        out_specs=pl.BlockSpec((lc, dv), lambda b, h, c: (b * nc + c, h)),
        scratch_shapes=[pltpu.VMEM((dk, dv), F32), pltpu.VMEM((1, dk), F32),
                        pltpu.VMEM((1, 1), F32), pltpu.VMEM((1, 1), F32)],
        compiler_params=_params(3),
        name=name,
    )(qkv, qkv, qkv, o, gate_cols, gate_cols, gate_rows, out_gain.reshape(heads, 1, dv))


def _rope_tables(seq, dim):
    pos = jnp.arange(seq, dtype=F32)
    inv = ROPE_THETA ** (-jnp.arange(0, dim, 2, dtype=F32) / dim)
    ang = pos[:, None] * inv[None, :]
    return jnp.cos(ang), jnp.sin(ang)


def _ffn(x, g, w_in, w_out, name):
    h = rmsnorm(x, g, name + "_norm")
    act = mm_swiglu(h, w_in.astype(BF16), name + "_in")
    return mm_residual(act, w_out.astype(BF16), x, 0.5, name + "_out")


def _gate_arrays(gates, n_rows, batch, seq):
    rows = gates.reshape(batch, seq, V7X_LANES)[:, :, :n_rows].transpose(0, 2, 1)
    return rows[..., None], rows


def _fox(x, h, w_in, q_gain, k_gain, f_bias, w_out, batch, seq, name):
    d = h.shape[1]
    hd = d // FOX_HEADS
    gain_row = jnp.concatenate([jnp.tile(q_gain, FOX_HEADS), jnp.tile(k_gain, FOX_HEADS)])[None, :]
    qk = mm_headnorm(h, w_in[:, :2 * d].astype(BF16), gain_row.astype(F32), name + "_qk")
    v = mm_plain(h, w_in[:, 2 * d:3 * d].astype(BF16), BF16, name + "_v")
    f = mm_plain(h, _pad_lanes(w_in[:, 3 * d:]).astype(BF16), F32, name + "_f", tn=V7X_LANES)
    mask = (jnp.arange(V7X_LANES) < FOX_HEADS).astype(F32)[None, :]
    cum = gate_scan(f, _pad_lanes(f_bias[None, :]), mask, batch, seq, name + "_scan")
    cum_cols, cum_rows = _gate_arrays(cum, FOX_HEADS, batch, seq)
    o = flash_attention(qk, qk, v, batch=batch, seq=seq, heads=FOX_HEADS, n_maps=1, dk=hd, dv=hd,
                        q_blk_off=0, k_blk_off=FOX_HEADS, v_blk_off=0, frame_causal=True,
                        scale=hd ** -0.5, decay_cols=cum_cols, decay_rows=cum_rows[:, :, None, :],
                        name=name + "_attn")
    return mm_residual(o, w_out.astype(BF16), x, 1.0, name + "_out")


def _diff(x, h, w_in, q_gain, k_gain, lam_p, sub_gain, w_out, lam_init, batch, seq, name):
    d = h.shape[1]
    hd = d // (2 * DIFF_HEADS)
    dv = 2 * hd
    n_maps = 2 * DIFF_HEADS
    cos, sin = _rope_tables(seq, hd)
    tabs = (jnp.concatenate([cos, cos], -1), jnp.concatenate([-sin, sin], -1))
    gain_row = jnp.concatenate([jnp.tile(q_gain, n_maps), jnp.tile(k_gain, n_maps)])[None, :]
    qk = mm_headnorm(h, w_in[:, :2 * d].astype(BF16), gain_row.astype(F32), name + "_qk",
                     rope_tabs=tabs, seq=seq)
    v = mm_plain(h, w_in[:, 2 * d:].astype(BF16), BF16, name + "_v")
    const = lambda shape: pl.BlockSpec(shape, lambda b, hh, i: (0,) * len(shape))
    o = flash_attention(qk, qk, v, batch=batch, seq=seq, heads=DIFF_HEADS, n_maps=2, dk=hd, dv=dv,
                        q_blk_off=0, k_blk_off=DIFF_HEADS, v_blk_off=0, frame_causal=False,
                        scale=hd ** -0.5, extra=[lam_p.astype(F32), sub_gain[None, :].astype(F32)],
                        extra_specs=[const((4, hd)), const((1, dv))],
                        finalize=functools.partial(_fin_diff, lam_init=lam_init),
                        name=name + "_attn")
    return mm_residual(o, w_out.astype(BF16), x, 1.0, name + "_out")


def _mla_rope_layout(t):
    half = MLA_ROPE_DIM // 2
    z = jnp.zeros(t.shape[:-1] + (V7X_LANES // 2 - half,), t.dtype)
    return jnp.concatenate([t[..., :half], z, t[..., half:], z], axis=-1)


def _mla(x, h, w_in, q_lat_gain, w_q_up, kv_lat_gain, w_kv_up, q_gain, k_gain, w_out,
         batch, seq, name):
    nh, dn, dr, dv = MLA_HEADS, MLA_NOPE_DIM, MLA_ROPE_DIM, MLA_V_DIM
    q_rank, kv_rank = w_q_up.shape[0], w_kv_up.shape[0]
    cos, sin = _rope_tables(seq, dr)
    z = jnp.zeros_like(cos)
    tabs = (jnp.concatenate([cos, z, cos, z], -1), jnp.concatenate([-sin, z, sin, z], -1))

    cq = mm_rownorm(h, w_in[:, :q_rank].astype(BF16), q_lat_gain[None, :].astype(F32), name + "_cq")
    ckv = mm_rownorm(h, w_in[:, q_rank:q_rank + kv_rank].astype(BF16),
                     kv_lat_gain[None, :].astype(F32), name + "_ckv")
    kpe = mm_plain(h, _mla_rope_layout(w_in[:, q_rank + kv_rank:]).astype(BF16), F32,
                   name + "_kpe", tn=V7X_LANES)

    def pad_head(t):
        return jnp.concatenate([t[..., :dn], _mla_rope_layout(t[..., dn:])], axis=-1)

    wq = pad_head(w_q_up.reshape(q_rank, nh, dn + dr)).reshape(q_rank, nh * 2 * V7X_LANES)
    t = h.shape[0]
    tm = _tile(t, MM_TM)
    rs = _rope_specs(t, seq, tm)
    qshape, qspec = _out_tile(t, wq.shape[1], BF16, tm, MM_TN)
    row256 = pl.BlockSpec((1, 2 * V7X_LANES), lambda i, j: (0, 0))
    row128 = pl.BlockSpec((1, V7X_LANES), lambda i, j: (0, 0))
    q = matmul(cq, [wq.astype(BF16)], [0], wq.shape[1], tm=tm, tn=MM_TN,
               epilogue=functools.partial(_ep_mla_q, n_valid=dn + dr),
               out_shapes=qshape, out_specs=qspec,
               aux=[pad_head(q_gain)[None, :].astype(F32), tabs[0], tabs[1]],
               aux_specs=[row256, rs, rs], name=name + "_q")

    n_kv = w_kv_up.shape[1]
    tn = _tile(n_kv, MM_TN)
    kshape, kspec = _out_tile(t, n_kv, BF16, tm, tn)
    vshape = jax.ShapeDtypeStruct((t, n_kv // 2), BF16)
    vspec = pl.BlockSpec((tm, tn // 2), lambda i, j: (i, j))
    k, v = matmul(ckv, [w_kv_up.astype(BF16)], [0], n_kv, tm=tm, tn=tn,
                  epilogue=functools.partial(_ep_mla_kv, n_valid=dn + dr),
                  out_shapes=(kshape, vshape), out_specs=(kspec, vspec),
                  aux=[kpe, k_gain[None, :dn].astype(F32),
                       _mla_rope_layout(k_gain[dn:])[None, :].astype(F32), tabs[0], tabs[1]],
                  aux_specs=[pl.BlockSpec((tm, V7X_LANES), lambda i, j: (i, 0)), row128, row128, rs, rs],
                  name=name + "_kv")
    o = flash_attention(q, k, v, batch=batch, seq=seq, heads=nh, n_maps=1, dk=2 * V7X_LANES, dv=dv,
                        q_blk_off=0, k_blk_off=0, v_blk_off=0, frame_causal=False,
                        scale=(dn + dr) ** -0.5, name=name + "_attn")
    return mm_residual(o, w_out.astype(BF16), x, 1.0, name + "_out")


def _mlstm(x, h, w_in, gate_bias, out_gain, w_out, batch, seq, name):
    d = h.shape[1]
    nh = MLSTM_HEADS
    dk, dv = d // (2 * nh), d // nh
    n_qkv = 2 * nh * dk + nh * dv
    qkv = mm_plain(h, w_in[:, :n_qkv].astype(BF16), BF16, name + "_qkv")
    o = mm_plain(h, w_in[:, n_qkv:n_qkv + d].astype(BF16), F32, name + "_o")
    gates = mm_plain(h, _pad_lanes(w_in[:, n_qkv + d:]).astype(BF16), F32, name + "_g", tn=V7X_LANES)
    bias = _pad_lanes(jnp.concatenate([gate_bias[0], gate_bias[1]])[None, :])
    lane = jnp.arange(V7X_LANES)
    mask = ((lane >= nh) & (lane < 2 * nh)).astype(F32)[None, :]
    scanned = gate_scan(gates, bias, mask, batch, seq, name + "_scan")
    gate_cols, gate_rows = _gate_arrays(scanned, 2 * nh, batch, seq)
    y = mlstm_chunks(qkv, o, gate_cols, gate_rows, out_gain.astype(F32), batch=batch, seq=seq,
                     heads=nh, dk=dk, dv=dv, name=name + "_cell")
    return mm_residual(y, w_out.astype(BF16), x, 1.0, name + "_out")


def kernel(x, norm_ffn, ffn_w_in, ffn_w_out, norm_mix, fox_w_in, fox_q_gain, fox_k_gain, fox_f_bias, fox_w_out, diff_w_in, diff_q_gain, diff_k_gain, diff_lambda, diff_sub_gain, diff_w_out, mla_w_in, mla_q_lat_gain, mla_w_q_up, mla_kv_lat_gain, mla_w_kv_up, mla_q_gain, mla_k_gain, mla_w_out, mlstm_w_in, mlstm_gate_bias, mlstm_out_gain, mlstm_w_out):
    batch, seq, d = x.shape
    depth = norm_mix.shape[0]
    x = x.reshape(batch * seq, d)
    for i in range(depth):
        kind, occ = i % N_MIXERS, i // N_MIXERS
        x = _ffn(x, norm_ffn[i, 0], ffn_w_in[i, 0], ffn_w_out[i, 0], f"l{i}_ffa")
        h = rmsnorm(x, norm_mix[i], f"l{i}_mixnorm")
        if kind == 0:
            x = _fox(x, h, fox_w_in[occ], fox_q_gain[occ], fox_k_gain[occ], fox_f_bias[occ],
                     fox_w_out[occ], batch, seq, f"l{i}_fox")
        elif kind == 1:
            lam_init = 0.8 - 0.6 * math.exp(-0.3 * i)
            x = _diff(x, h, diff_w_in[occ], diff_q_gain[occ], diff_k_gain[occ], diff_lambda[occ],
                      diff_sub_gain[occ], diff_w_out[occ], lam_init, batch, seq, f"l{i}_diff")
        elif kind == 2:
            x = _mla(x, h, mla_w_in[occ], mla_q_lat_gain[occ], mla_w_q_up[occ], mla_kv_lat_gain[occ],
                     mla_w_kv_up[occ], mla_q_gain[occ], mla_k_gain[occ], mla_w_out[occ],
                     batch, seq, f"l{i}_mla")
        else:
            x = _mlstm(x, h, mlstm_w_in[occ], mlstm_gate_bias[occ], mlstm_out_gain[occ],
                       mlstm_w_out[occ], batch, seq, f"l{i}_mlstm")
        x = _ffn(x, norm_ffn[i, 1], ffn_w_in[i, 1], ffn_w_out[i, 1], f"l{i}_ffb")
    return x.reshape(batch, seq, d)
```

```python
import functools
import math

import jax
import jax.numpy as jnp
from jax import lax
from jax.experimental import pallas as pl
from jax.experimental.pallas import tpu as pltpu

F32 = jnp.float32
BF16 = jnp.bfloat16

CHUNK = 64
EPS = 1e-6
ROPE_THETA = 10000.0
LOG2E = math.log2(math.e)
N_MIXERS = 4
FOX_HEADS = 32
DIFF_HEADS = 16
MLA_HEADS = 32
MLA_NOPE_DIM = 128
MLA_ROPE_DIM = 64
MLA_V_DIM = 128
MLSTM_HEADS = 8

V7X_LANES = 128
V7X_VMEM_LIMIT_BYTES = 56 * 1024 * 1024

MM_TM = 1024
MM_TN = 1024
ATTN_TQ = 512
ATTN_TK = 512
MLSTM_LC = 256
NORM_TR = 256
SCAN_BLK = 512


def _tile(n, pref):
    t = min(n, pref)
    assert n % t == 0, (n, pref)
    return t


def _params(n_grid):
    return pltpu.CompilerParams(
        dimension_semantics=("arbitrary",) * n_grid,
        vmem_limit_bytes=V7X_VMEM_LIMIT_BYTES,
    )


def _rmsnorm_body(x_ref, g_ref, o_ref):
    x = x_ref[...]
    ms = jnp.mean(x * x, axis=-1, keepdims=True)
    o_ref[...] = (x * lax.rsqrt(ms + EPS) * g_ref[...]).astype(o_ref.dtype)


def rmsnorm(x, g, name):
    t, d = x.shape
    tr = _tile(t, NORM_TR)
    return pl.pallas_call(
        _rmsnorm_body,
        out_shape=jax.ShapeDtypeStruct((t, d), BF16),
        grid=(t // tr,),
        in_specs=[pl.BlockSpec((tr, d), lambda i: (i, 0)),
                  pl.BlockSpec((1, d), lambda i: (0, 0))],
        out_specs=pl.BlockSpec((tr, d), lambda i: (i, 0)),
        compiler_params=_params(1),
        name=name,
    )(x, g.reshape(1, d).astype(F32))


def _mm_body(*refs, n_w, n_aux, epilogue):
    a_ref = refs[0]
    w_refs = refs[1:1 + n_w]
    aux_refs = refs[1 + n_w:1 + n_w + n_aux]
    out_refs = refs[1 + n_w + n_aux:]
    a = a_ref[...]
    accs = [jnp.dot(a, w[...], preferred_element_type=F32) for w in w_refs]
    epilogue(accs, aux_refs, out_refs)


def matmul(a, ws, w_blk_offsets, n_cols, *, tm, tn, epilogue, out_shapes, out_specs,
           aux=(), aux_specs=(), name):
    t, k = a.shape
    tm = _tile(t, tm)
    tn = _tile(n_cols, tn)
    in_specs = [pl.BlockSpec((tm, k), lambda i, j: (i, 0))]
    for off in w_blk_offsets:
        in_specs.append(pl.BlockSpec((k, tn), lambda i, j, off=off: (0, j + off)))
    in_specs += list(aux_specs)
    body = functools.partial(_mm_body, n_w=len(ws), n_aux=len(aux), epilogue=epilogue)
    return pl.pallas_call(
        body,
        out_shape=out_shapes,
        grid=(t // tm, n_cols // tn),
        in_specs=in_specs,
        out_specs=out_specs,
        compiler_params=_params(2),
        name=name,
    )(a, *ws, *aux)


def _ep_plain(accs, aux, outs):
    outs[0][...] = accs[0].astype(outs[0].dtype)


def _ep_swiglu(accs, aux, outs):
    g, u = accs
    outs[0][...] = (g * jax.nn.sigmoid(g) * u).astype(outs[0].dtype)


def _ep_residual(accs, aux, outs, *, alpha):
    outs[0][...] = aux[0][...] + alpha * accs[0]


def _rope(t, cosf, sinf):
    return t * cosf + pltpu.roll(t, 64, 1) * sinf


def _ep_headnorm(accs, aux, outs, *, rope):
    acc = accs[0]
    if rope:
        cosf, sinf = aux[1][...], aux[2][...]
    for c in range(acc.shape[1] // V7X_LANES):
        sl = slice(c * V7X_LANES, (c + 1) * V7X_LANES)
        t = acc[:, sl]
        y = t * lax.rsqrt(jnp.mean(t * t, axis=-1, keepdims=True) + EPS) * aux[0][:, sl]
        if rope:
            y = _rope(y, cosf, sinf)
        outs[0][:, sl] = y.astype(outs[0].dtype)


def _ep_rownorm(accs, aux, outs):
    t = accs[0]
    y = t * lax.rsqrt(jnp.mean(t * t, axis=-1, keepdims=True) + EPS) * aux[0][...]
    outs[0][...] = y.astype(outs[0].dtype)


def _ep_mla_q(accs, aux, outs, *, n_valid):
    acc = accs[0]
    gain, cosf, sinf = aux[0][...], aux[1][...], aux[2][...]
    w = 2 * V7X_LANES
    for c in range(acc.shape[1] // w):
        t = acc[:, c * w:(c + 1) * w]
        r = lax.rsqrt(jnp.sum(t * t, axis=-1, keepdims=True) / n_valid + EPS)
        y = t * r * gain
        o = c * w
        outs[0][:, o:o + V7X_LANES] = y[:, :V7X_LANES].astype(outs[0].dtype)
        outs[0][:, o + V7X_LANES:o + w] = _rope(y[:, V7X_LANES:], cosf, sinf).astype(outs[0].dtype)


def _ep_mla_kv(accs, aux, outs, *, n_valid):
    acc = accs[0]
    kpe, gain_nope, gain_pe, cosf, sinf = (r[...] for r in aux)
    k_out, v_out = outs
    ssq_pe = jnp.sum(kpe * kpe, axis=-1, keepdims=True)
    pe_rot = _rope(kpe * gain_pe, cosf, sinf)
    w = 2 * V7X_LANES
    for c in range(acc.shape[1] // w):
        kn = acc[:, c * w:c * w + V7X_LANES]
        v = acc[:, c * w + V7X_LANES:(c + 1) * w]
        r = lax.rsqrt((jnp.sum(kn * kn, axis=-1, keepdims=True) + ssq_pe) / n_valid + EPS)
        o = c * w
        k_out[:, o:o + V7X_LANES] = (kn * r * gain_nope).astype(k_out.dtype)
        k_out[:, o + V7X_LANES:o + w] = (pe_rot * r).astype(k_out.dtype)
        v_out[:, o // 2:o // 2 + V7X_LANES] = v.astype(v_out.dtype)


def _out_tile(t, n, dtype, tm, tn):
    tm, tn = _tile(t, tm), _tile(n, tn)
    return jax.ShapeDtypeStruct((t, n), dtype), pl.BlockSpec((tm, tn), lambda i, j: (i, j))


def _col_blk(col_off, n, tn):
    tn = _tile(n, tn)
    assert col_off % tn == 0, (col_off, tn)
    return col_off // tn


def mm_plain(a, w, out_dtype, name, n_cols=None, col_off=0, tm=MM_TM, tn=MM_TN):
    t, n = a.shape[0], n_cols or w.shape[1]
    shape, spec = _out_tile(t, n, out_dtype, tm, tn)
    return matmul(a, [w], [_col_blk(col_off, n, tn)], n, tm=tm, tn=tn, epilogue=_ep_plain,
                  out_shapes=shape, out_specs=spec, name=name)


def mm_swiglu(a, w_in, name, tm=MM_TM, tn=MM_TN // 2):
    t, n = a.shape[0], w_in.shape[1] // 2
    tn = _tile(n, tn)
    shape, spec = _out_tile(t, n, BF16, tm, tn)
    return matmul(a, [w_in, w_in], [0, n // tn], n, tm=tm, tn=tn, epilogue=_ep_swiglu,
                  out_shapes=shape, out_specs=spec, name=name)


def mm_residual(a, w, res, alpha, name, tm=MM_TM, tn=MM_TN // 2):
    t, n = a.shape[0], w.shape[1]
    shape, spec = _out_tile(t, n, F32, tm, tn)
    return matmul(a, [w], [0], n, tm=tm, tn=tn,
                  epilogue=functools.partial(_ep_residual, alpha=alpha),
                  out_shapes=shape, out_specs=spec, aux=[res], aux_specs=[spec], name=name)


def _rope_specs(t, seq, tm):
    tm = _tile(t, tm)
    assert seq % tm == 0
    nblk = seq // tm
    return pl.BlockSpec((tm, V7X_LANES), lambda i, j: (i % nblk, 0))


def mm_headnorm(a, w, gain_row, name, rope_tabs=None, seq=None, tm=MM_TM, tn=MM_TN):
    t, n = a.shape[0], gain_row.shape[1]
    tn_ = _tile(n, tn)
    shape, spec = _out_tile(t, n, BF16, tm, tn)
    aux = [gain_row]
    aux_specs = [pl.BlockSpec((1, tn_), lambda i, j: (0, j))]
    if rope_tabs is not None:
        rs = _rope_specs(t, seq, tm)
        aux += list(rope_tabs)
        aux_specs += [rs, rs]
    return matmul(a, [w], [0], n, tm=tm, tn=tn,
                  epilogue=functools.partial(_ep_headnorm, rope=rope_tabs is not None),
                  out_shapes=shape, out_specs=spec, aux=aux, aux_specs=aux_specs,
                  name=name)


def mm_rownorm(a, w, gain_row, name, col_off=0, tm=MM_TM):
    t, n = a.shape[0], gain_row.shape[1]
    shape, spec = _out_tile(t, n, BF16, tm, n)
    return matmul(a, [w], [_col_blk(col_off, n, n)], n, tm=tm, tn=n, epilogue=_ep_rownorm,
                  out_shapes=shape, out_specs=spec, aux=[gain_row],
                  aux_specs=[pl.BlockSpec((1, n), lambda i, j: (0, 0))], name=name)


def _gate_scan_body(x_ref, bias_ref, mult_ref, o_ref, carry_ref, *, blk):
    @pl.when(pl.program_id(1) == 0)
    def _():
        carry_ref[...] = jnp.zeros_like(carry_ref)

    y = x_ref[...] + bias_ref[...]
    c = jnp.minimum(y, 0.0) - jnp.log1p(jnp.exp(-jnp.abs(y)))
    row = lax.broadcasted_iota(jnp.int32, c.shape, 0)
    shift = 1
    while shift < blk:
        c = c + jnp.where(row >= shift, pltpu.roll(c, shift, 0), 0.0)
        shift *= 2
    c = c + carry_ref[...]
    carry_ref[...] = c[blk - 1:blk, :]
    mult = mult_ref[...]
    o_ref[...] = jnp.where(mult != 0.0, c * mult, y)


def gate_scan(x, bias_row, mult_row, batch, seq, name):
    blk = _tile(seq, SCAN_BLK)
    nblk = seq // blk
    spec = pl.BlockSpec((blk, V7X_LANES), lambda b, s: (b * nblk + s, 0))
    row = pl.BlockSpec((1, V7X_LANES), lambda b, s: (0, 0))
    return pl.pallas_call(
        functools.partial(_gate_scan_body, blk=blk),
        out_shape=jax.ShapeDtypeStruct(x.shape, F32),
        grid=(batch, nblk),
        in_specs=[spec, row, row],
        out_specs=spec,
        scratch_shapes=[pltpu.VMEM((1, V7X_LANES), F32)],
        compiler_params=_params(2),
        name=name,
    )(x, bias_row, mult_row)


def _pad_lanes(v, width=V7X_LANES):
    return jnp.pad(v, [(0, 0)] * (v.ndim - 1) + [(0, width - v.shape[-1])])


class _Stream:
    def __init__(self, q_cols, k_cols, head):
        self.q_cols, self.k_cols, self.head = q_cols, k_cols, head


def _flash_body(*refs, streams, hp, dv, tq, tk, seq, frame_causal, decay, n_extra, finalize):
    q_ref, k_ref, v_ref = refs[:3]
    pos = 3
    if decay:
        cq_ref, ck_ref = refs[3:5]
        pos = 5
    extra = refs[pos:pos + n_extra]
    o_ref = refs[pos + n_extra]
    vt_ref = refs[pos + n_extra + 1]
    ckt_ref = refs[pos + n_extra + 2] if decay else None
    qi = pl.program_id(2)

    @pl.when(qi == 0)
    def _():
        def fill(c, _):
            start = pl.multiple_of(c * tq, tq)
            for e in range(hp):
                vt_ref[e, :, pl.ds(start, tq)] = v_ref[pl.ds(start, tq), e * dv:(e + 1) * dv].T
                if decay:
                    ck = jnp.broadcast_to(ck_ref[e, :, pl.ds(start, tq)], (V7X_LANES, tq))
                    ckt_ref[e, pl.ds(start, tq), :] = ck.T
            return 0
        lax.fori_loop(0, seq // tq, fill, 0)

    def diag_mask(d):
        key = lax.broadcasted_iota(jnp.int32, (tk, tq), 0) + d * tk
        qry = lax.broadcasted_iota(jnp.int32, (tk, tq), 1)
        if frame_causal:
            return key <= qry
        shift = CHUNK.bit_length() - 1
        return jnp.right_shift(key, shift) <= jnp.right_shift(qry, shift)

    qs = [q_ref[:, st.q_cols] for st in streams]
    cqs = [cq_ref[st.head] if decay else None for st in streams]

    def block(j, state, mask):
        start = pl.multiple_of(j * tk, tk)
        ts = [lax.dot_general(k_ref[pl.ds(start, tk), st.k_cols], q, (((1,), (1,)), ((), ())),
                              preferred_element_type=F32) for st, q in zip(streams, qs)]
        out = []
        for st, t, cq, (m, l, acc) in zip(streams, ts, cqs, state):
            vt = vt_ref[st.head, :, pl.ds(start, tk)]
            if decay:
                ckb = ckt_ref[st.head, pl.ds(start, tk), :]
                t = t - jnp.concatenate([ckb] * (tq // V7X_LANES), axis=1)
            if mask is not None:
                t = jnp.where(mask, t, -jnp.inf)
            cmax = jnp.max(t, axis=0, keepdims=True)
            if decay:
                m_new = jnp.maximum(m, cmax + cq)
                mt = m_new - cq
            else:
                m_new = jnp.maximum(m, cmax)
                mt = m_new
            alpha = jnp.exp2(m - m_new)
            p = jnp.exp2(t - mt)
            l = alpha * l + jnp.sum(p, axis=0, keepdims=True)
            acc = alpha * acc + jnp.dot(vt, p.astype(BF16), preferred_element_type=F32)
            out.append((m_new, l, acc))
        return tuple(out)

    init = tuple((jnp.full((1, tq), -jnp.inf, F32), jnp.zeros((1, tq), F32),
                  jnp.zeros((dv, tq), F32)) for _ in streams)
    n_diag = tq // tk
    n_full = qi * n_diag
    state = lax.fori_loop(0, n_full, lambda j, s: block(j, s, None), init)
    for d in range(n_diag):
        state = block(n_full + d, state, diag_mask(d))
    finalize([acc / l for _, l, acc in state], extra, o_ref)


def _fin_heads(outs, extra, o_ref):
    dv = outs[0].shape[0]
    for e, o in enumerate(outs):
        o_ref[:, e * dv:(e + 1) * dv] = o.T.astype(o_ref.dtype)


def _fin_diff(outs, extra, o_ref, *, lam_init):
    lp = extra[0][...]
    sub_gain = extra[1][...]
    lam = (jnp.exp(jnp.sum(lp[0:1] * lp[1:2], axis=-1, keepdims=True))
           - jnp.exp(jnp.sum(lp[2:3] * lp[3:4], axis=-1, keepdims=True)) + lam_init)
    dv = outs[0].shape[0]
    for e in range(len(outs) // 2):
        y = (outs[2 * e] - lam * outs[2 * e + 1]).T
        y = y * lax.rsqrt(jnp.mean(y * y, axis=-1, keepdims=True) + EPS) * sub_gain
        o_ref[:, e * dv:(e + 1) * dv] = (y * (1.0 - lam_init)).astype(o_ref.dtype)


def flash_attention(q_arr, k_arr, v_arr, *, batch, seq, heads, hp, n_maps, dk, dv,
                    q_blk_off, k_blk_off, v_blk_off, frame_causal,
                    decay_rows=None, extra=(), extra_specs=(), finalize=_fin_heads, name):
    tq = _tile(seq, ATTN_TQ)
    tk = _tile(tq, ATTN_TK)
    assert tk % CHUNK == 0 and tk % V7X_LANES == 0 and heads % hp == 0
    nq = seq // tq
    qw = hp * n_maps * dk
    vw = hp * dv
    decay = decay_rows is not None
    streams = []
    for e in range(hp):
        for g in range(n_maps):
            c = (e * n_maps + g) * dk
            streams.append(_Stream(slice(c, c + dk), slice(c, c + dk), e))
    in_specs = [
        pl.BlockSpec((tq, qw), lambda b, h, i: (b * nq + i, h + q_blk_off)),
        pl.BlockSpec((seq, qw), lambda b, h, i: (b, h + k_blk_off)),
        pl.BlockSpec((seq, vw), lambda b, h, i: (b, h + v_blk_off)),
    ]
    args = [q_arr, k_arr, v_arr]
    scratch = [pltpu.VMEM((hp, dv, seq), BF16)]
    if decay:
        in_specs += [pl.BlockSpec((None, hp, 1, tq), lambda b, h, i: (b, h, 0, i)),
                     pl.BlockSpec((None, hp, 1, seq), lambda b, h, i: (b, h, 0, 0))]
        args += [decay_rows, decay_rows]
        scratch.append(pltpu.VMEM((hp, seq, V7X_LANES), F32))
    in_specs += list(extra_specs)
    args += list(extra)
    body = functools.partial(_flash_body, streams=streams, hp=hp, dv=dv, tq=tq, tk=tk, seq=seq,
                             frame_causal=frame_causal, decay=decay,
                             n_extra=len(extra), finalize=finalize)
    return pl.pallas_call(
        body,
        out_shape=jax.ShapeDtypeStruct((batch * seq, heads * dv), BF16),
        grid=(batch, heads // hp, nq),
        in_specs=in_specs,
        out_specs=pl.BlockSpec((tq, vw), lambda b, h, i: (b * nq + i, h)),
        scratch_shapes=scratch,
        compiler_params=_params(3),
        name=name,
    )(*args)


def _mlstm_body(q_ref, k_ref, v_ref, o_ref, ipc_ref, cumc_ref, gr_ref, gain_ref, y_ref,
                c_ref, n_ref, m_ref, cprev_ref, *, lc, heads, dk):
    h = pl.program_id(1)

    @pl.when(pl.program_id(2) == 0)
    def _():
        c_ref[...] = jnp.zeros_like(c_ref)
        n_ref[...] = jnp.zeros_like(n_ref)
        m_ref[...] = jnp.zeros_like(m_ref)
        cprev_ref[...] = jnp.zeros_like(cprev_ref)

    qscale = dk ** -0.5
    q = q_ref[...]
    k = k_ref[...]
    v = v_ref[...]
    ip_col = ipc_ref[...]
    ip_row = gr_ref[pl.ds(h, 1), :]
    cum_row = gr_ref[pl.ds(heads + h, 1), :]
    cprev = cprev_ref[...]
    m_prev = m_ref[...]
    b_col = cumc_ref[...] - cprev
    b_row = cum_row - cprev
    b_last = b_row[:, lc - 1:lc]

    row = lax.broadcasted_iota(jnp.int32, (lc, lc), 0)
    col = lax.broadcasted_iota(jnp.int32, (lc, lc), 1)
    d = jnp.where(col <= row, b_col - b_row + ip_row, -jnp.inf)
    inter = b_col + m_prev
    m_row = jnp.maximum(jnp.max(d, axis=-1, keepdims=True), inter)
    w_intra = jnp.exp(d - m_row)
    w_inter = jnp.exp(inter - m_row)

    qk = lax.dot_general(q, k, (((1,), (1,)), ((), ())), preferred_element_type=F32) * qscale
    a = w_intra * qk
    c_old = c_ref[...]
    num = (jnp.dot(a.astype(BF16), v, preferred_element_type=F32)
           + w_inter * (jnp.dot(q, c_old.astype(BF16), preferred_element_type=F32) * qscale))
    qn = jnp.sum(q.astype(F32) * n_ref[...], axis=-1, keepdims=True) * qscale
    den = jnp.sum(a, axis=-1, keepdims=True) + w_inter * qn
    hc = num / jnp.maximum(jnp.abs(den), jnp.exp(-m_row))

    g_row = b_last - b_row + ip_row
    g_col = b_last - b_col + ip_col
    m_new = jnp.maximum(b_last + m_prev, jnp.max(g_row, axis=-1, keepdims=True))
    decay = jnp.exp(b_last + m_prev - m_new)
    kw = k.astype(F32) * jnp.exp(g_col - m_new)
    c_ref[...] = decay * c_old + jnp.dot(kw.T.astype(BF16), v, preferred_element_type=F32)
    n_ref[...] = decay * n_ref[...] + jnp.sum(kw, axis=0, keepdims=True)
    m_ref[...] = m_new
    cprev_ref[...] = cum_row[:, lc - 1:lc]

    y = hc * lax.rsqrt(jnp.mean(hc * hc, axis=-1, keepdims=True) + EPS) * gain_ref[...]
    y_ref[...] = (y * jax.nn.sigmoid(o_ref[...])).astype(y_ref.dtype)


def mlstm_chunks(qkv, o, gate_cols, gate_rows, out_gain, *, batch, seq, heads, dk, dv, name):
    lc = _tile(seq, MLSTM_LC)
    nc = seq // lc
    kb, vb = heads, (2 * heads * dk) // dv
    col_spec = lambda off: pl.BlockSpec((None, None, lc, 1), lambda b, h, c: (b, h + off, c, 0))
    in_specs = [
        pl.BlockSpec((lc, dk), lambda b, h, c: (b * nc + c, h)),
        pl.BlockSpec((lc, dk), lambda b, h, c: (b * nc + c, kb + h)),
        pl.BlockSpec((lc, dv), lambda b, h, c: (b * nc + c, vb + h)),
        pl.BlockSpec((lc, dv), lambda b, h, c: (b * nc + c, h)),
        col_spec(0),
        col_spec(heads),
        pl.BlockSpec((None, 2 * heads, lc), lambda b, h, c: (b, 0, c)),
        pl.BlockSpec((None, 1, dv), lambda b, h, c: (h, 0, 0)),
    ]
    return pl.pallas_call(
        functools.partial(_mlstm_body, lc=lc, heads=heads, dk=dk),
        out_shape=jax.ShapeDtypeStruct((batch * seq, heads * dv), BF16),
        grid=(batch, heads, nc),
        in_specs=in_specs,
        out_specs=pl.BlockSpec((lc, dv), lambda b, h, c: (b * nc + c, h)),
        scratch_shapes=[pltpu.VMEM((dk, dv), F32), pltpu.VMEM((1, dk), F32),
                        pltpu.VMEM((1, 1), F32), pltpu.VMEM((1, 1), F32)],
        compiler_params=_params(3),
        name=name,
    )(qkv, qkv, qkv, o, gate_cols, gate_cols, gate_rows, out_gain.reshape(heads, 1, dv))


def _rope_tables(seq, dim):
    pos = jnp.arange(seq, dtype=F32)
    inv = ROPE_THETA ** (-jnp.arange(0, dim, 2, dtype=F32) / dim)
    ang = pos[:, None] * inv[None, :]
    return jnp.cos(ang), jnp.sin(ang)


def _ffn(x, g, w_in, w_out, name):
    h = rmsnorm(x, g, name + "_norm")
    act = mm_swiglu(h, w_in.astype(BF16), name + "_in")
    return mm_residual(act, w_out.astype(BF16), x, 0.5, name + "_out")


def _gate_arrays(gates, n_rows, batch, seq):
    rows = gates.reshape(batch, seq, V7X_LANES)[:, :, :n_rows].transpose(0, 2, 1)
    return rows[..., None], rows


def _fox(x, h, w_in, q_gain, k_gain, f_bias, w_out, batch, seq, name):
    d = h.shape[1]
    hd = d // FOX_HEADS
    hp = 4
    q_scale = hd ** -0.5 * LOG2E
    gain_row = jnp.concatenate([jnp.tile(q_gain * q_scale, FOX_HEADS),
                                jnp.tile(k_gain, FOX_HEADS)])[None, :]
    wb = w_in.astype(BF16)
    qk = mm_headnorm(h, wb, gain_row.astype(F32), name + "_qk")
    v = mm_plain(h, wb, BF16, name + "_v", n_cols=d, col_off=2 * d)
    f = mm_plain(h, _pad_lanes(w_in[:, 3 * d:]).astype(BF16), F32, name + "_f", tn=V7X_LANES)
    mult = jnp.where(jnp.arange(V7X_LANES) < FOX_HEADS, LOG2E, 0.0).astype(F32)[None, :]
    cum = gate_scan(f, _pad_lanes(f_bias[None, :]), mult, batch, seq, name + "_scan")
    _, cum_rows = _gate_arrays(cum, FOX_HEADS, batch, seq)
    o = flash_attention(qk, qk, v, batch=batch, seq=seq, heads=FOX_HEADS, hp=hp, n_maps=1,
                        dk=hd, dv=hd, q_blk_off=0, k_blk_off=FOX_HEADS // hp, v_blk_off=0,
                        frame_causal=True, decay_rows=cum_rows[:, :, None, :],
                        name=name + "_attn")
    return mm_residual(o, w_out.astype(BF16), x, 1.0, name + "_out")


def _diff(x, h, w_in, q_gain, k_gain, lam_p, sub_gain, w_out, lam_init, batch, seq, name):
    d = h.shape[1]
    hd = d // (2 * DIFF_HEADS)
    dv = 2 * hd
    n_maps = 2 * DIFF_HEADS
    cos, sin = _rope_tables(seq, hd)
    tabs = (jnp.concatenate([cos, cos], -1), jnp.concatenate([-sin, sin], -1))
    q_scale = hd ** -0.5 * LOG2E
    gain_row = jnp.concatenate([jnp.tile(q_gain * q_scale, n_maps), jnp.tile(k_gain, n_maps)])[None, :]
    wb = w_in.astype(BF16)
    qk = mm_headnorm(h, wb, gain_row.astype(F32), name + "_qk", rope_tabs=tabs, seq=seq)
    v = mm_plain(h, wb, BF16, name + "_v", n_cols=DIFF_HEADS * dv, col_off=2 * d)
    const = lambda shape: pl.BlockSpec(shape, lambda b, hh, i: (0,) * len(shape))
    hp = 2
    o = flash_attention(qk, qk, v, batch=batch, seq=seq, heads=DIFF_HEADS, hp=hp, n_maps=2,
                        dk=hd, dv=dv, q_blk_off=0, k_blk_off=DIFF_HEADS // hp, v_blk_off=0,
                        frame_causal=False, extra=[lam_p.astype(F32), sub_gain[None, :].astype(F32)],
                        extra_specs=[const((4, hd)), const((1, dv))],
                        finalize=functools.partial(_fin_diff, lam_init=lam_init),
                        name=name + "_attn")
    return mm_residual(o, w_out.astype(BF16), x, 1.0, name + "_out")


def _mla_rope_layout(t):
    half = MLA_ROPE_DIM // 2
    z = jnp.zeros(t.shape[:-1] + (V7X_LANES // 2 - half,), t.dtype)
    return jnp.concatenate([t[..., :half], z, t[..., half:], z], axis=-1)


def _mla(x, h, w_in, q_lat_gain, w_q_up, kv_lat_gain, w_kv_up, q_gain, k_gain, w_out,
         batch, seq, name):
    nh, dn, dr, dv = MLA_HEADS, MLA_NOPE_DIM, MLA_ROPE_DIM, MLA_V_DIM
    q_rank, kv_rank = w_q_up.shape[0], w_kv_up.shape[0]
    cos, sin = _rope_tables(seq, dr)
    z = jnp.zeros_like(cos)
    tabs = (jnp.concatenate([cos, z, cos, z], -1), jnp.concatenate([-sin, z, sin, z], -1))

    wb = w_in.astype(BF16)
    cq = mm_rownorm(h, wb, q_lat_gain[None, :].astype(F32), name + "_cq")
    ckv = mm_rownorm(h, wb, kv_lat_gain[None, :].astype(F32), name + "_ckv", col_off=q_rank)
    kpe = mm_plain(h, _mla_rope_layout(w_in[:, q_rank + kv_rank:]).astype(BF16), F32,
                   name + "_kpe", tn=V7X_LANES)

    def pad_head(t):
        return jnp.concatenate([t[..., :dn], _mla_rope_layout(t[..., dn:])], axis=-1)

    wq = pad_head(w_q_up.reshape(q_rank, nh, dn + dr)).reshape(q_rank, nh * 2 * V7X_LANES)
    t = h.shape[0]
    tm = _tile(t, MM_TM)
    rs = _rope_specs(t, seq, tm)
    qshape, qspec = _out_tile(t, wq.shape[1], BF16, tm, MM_TN)
    row256 = pl.BlockSpec((1, 2 * V7X_LANES), lambda i, j: (0, 0))
    row128 = pl.BlockSpec((1, V7X_LANES), lambda i, j: (0, 0))
    q = matmul(cq, [wq.astype(BF16)], [0], wq.shape[1], tm=tm, tn=MM_TN,
               epilogue=functools.partial(_ep_mla_q, n_valid=dn + dr),
               out_shapes=qshape, out_specs=qspec,
               aux=[pad_head(q_gain * ((dn + dr) ** -0.5 * LOG2E))[None, :].astype(F32),
                    tabs[0], tabs[1]],
               aux_specs=[row256, rs, rs], name=name + "_q")

    n_kv = w_kv_up.shape[1]
    tn = _tile(n_kv, MM_TN)
    kshape, kspec = _out_tile(t, n_kv, BF16, tm, tn)
    vshape = jax.ShapeDtypeStruct((t, n_kv // 2), BF16)
    vspec = pl.BlockSpec((tm, tn // 2), lambda i, j: (i, j))
    k, v = matmul(ckv, [w_kv_up.astype(BF16)], [0], n_kv, tm=tm, tn=tn,
                  epilogue=functools.partial(_ep_mla_kv, n_valid=dn + dr),
                  out_shapes=(kshape, vshape), out_specs=(kspec, vspec),
                  aux=[kpe, k_gain[None, :dn].astype(F32),
                       _mla_rope_layout(k_gain[dn:])[None, :].astype(F32), tabs[0], tabs[1]],
                  aux_specs=[pl.BlockSpec((tm, V7X_LANES), lambda i, j: (i, 0)), row128, row128, rs, rs],
                  name=name + "_kv")
    o = flash_attention(q, k, v, batch=batch, seq=seq, heads=nh, hp=4, n_maps=1,
                        dk=2 * V7X_LANES, dv=dv, q_blk_off=0, k_blk_off=0, v_blk_off=0,
                        frame_causal=False, name=name + "_attn")
    return mm_residual(o, w_out.astype(BF16), x, 1.0, name + "_out")


def _mlstm(x, h, w_in, gate_bias, out_gain, w_out, batch, seq, name):
    d = h.shape[1]
    nh = MLSTM_HEADS
    dk, dv = d // (2 * nh), d // nh
    n_qkv = 2 * nh * dk + nh * dv
    wb = w_in.astype(BF16)
    qkv = mm_plain(h, wb, BF16, name + "_qkv", n_cols=n_qkv)
    o = mm_plain(h, wb, F32, name + "_o", n_cols=d, col_off=n_qkv)
    gates = mm_plain(h, _pad_lanes(w_in[:, n_qkv + d:]).astype(BF16), F32, name + "_g", tn=V7X_LANES)
    bias = _pad_lanes(jnp.concatenate([gate_bias[0], gate_bias[1]])[None, :])
    lane = jnp.arange(V7X_LANES)
    mult = ((lane >= nh) & (lane < 2 * nh)).astype(F32)[None, :]
    scanned = gate_scan(gates, bias, mult, batch, seq, name + "_scan")
    gate_cols, gate_rows = _gate_arrays(scanned, 2 * nh, batch, seq)
    y = mlstm_chunks(qkv, o, gate_cols, gate_rows, out_gain.astype(F32), batch=batch, seq=seq,
                     heads=nh, dk=dk, dv=dv, name=name + "_cell")
    return mm_residual(y, w_out.astype(BF16), x, 1.0, name + "_out")


def kernel(x, norm_ffn, ffn_w_in, ffn_w_out, norm_mix, fox_w_in, fox_q_gain, fox_k_gain, fox_f_bias, fox_w_out, diff_w_in, diff_q_gain, diff_k_gain, diff_lambda, diff_sub_gain, diff_w_out, mla_w_in, mla_q_lat_gain, mla_w_q_up, mla_kv_lat_gain, mla_w_kv_up, mla_q_gain, mla_k_gain, mla_w_out, mlstm_w_in, mlstm_gate_bias, mlstm_out_gain, mlstm_w_out):
    batch, seq, d = x.shape
    depth = norm_mix.shape[0]
    x = x.reshape(batch * seq, d)
    for i in range(depth):
        kind, occ = i % N_MIXERS, i // N_MIXERS
        x = _ffn(x, norm_ffn[i, 0], ffn_w_in[i, 0], ffn_w_out[i, 0], f"l{i}_ffa")
        h = rmsnorm(x, norm_mix[i], f"l{i}_mixnorm")
        if kind == 0:
            x = _fox(x, h, fox_w_in[occ], fox_q_gain[occ], fox_k_gain[occ], fox_f_bias[occ],
                     fox_w_out[occ], batch, seq, f"l{i}_fox")
        elif kind == 1:
            lam_init = 0.8 - 0.6 * math.exp(-0.3 * i)
            x = _diff(x, h, diff_w_in[occ], diff_q_gain[occ], diff_k_gain[occ], diff_lambda[occ],
                      diff_sub_gain[occ], diff_w_out[occ], lam_init, batch, seq, f"l{i}_diff")
        elif kind == 2:
            x = _mla(x, h, mla_w_in[occ], mla_q_lat_gain[occ], mla_w_q_up[occ], mla_kv_lat_gain[occ],
                     mla_w_kv_up[occ], mla_q_gain[occ], mla_k_gain[occ], mla_w_out[occ],
                     batch, seq, f"l{i}_mla")
        else:
            x = _mlstm(x, h, mlstm_w_in[occ], mlstm_gate_bias[occ], mlstm_out_gain[occ],
                       mlstm_w_out[occ], batch, seq, f"l{i}_mlstm")
        x = _ffn(x, norm_ffn[i, 1], ffn_w_in[i, 1], ffn_w_out[i, 1], f"l{i}_ffb")
    return x.reshape(batch, seq, d)
```

```python
import functools
import math
from typing import NamedTuple

import jax
import jax.numpy as jnp
from jax import lax
from jax.experimental import pallas as pl
from jax.experimental.pallas import tpu as pltpu

F32 = jnp.float32
BF16 = jnp.bfloat16

CHUNK = 64
EPS = 1e-6
ROPE_THETA = 10000.0
LOG2E = math.log2(math.e)
N_MIXERS = 4
FOX_HEADS = 32
DIFF_HEADS = 16
MLA_HEADS = 32
MLA_NOPE_DIM = 128
MLA_ROPE_DIM = 64
MLA_V_DIM = 128
MLSTM_HEADS = 8

V7X_LANES = 128
V7X_VMEM_LIMIT_BYTES = 56 * 1024 * 1024

MM_TM = 1024
MM_TN = 1024
ATTN_TQ = 512
ATTN_TK = 512
MLSTM_LC = 256
NORM_TR = 256
SCAN_BLK = 512


def _tile(n, pref):
    t = min(n, pref)
    assert n % t == 0, (n, pref)
    return t


def _params(n_grid):
    return pltpu.CompilerParams(
        dimension_semantics=("arbitrary",) * n_grid,
        vmem_limit_bytes=V7X_VMEM_LIMIT_BYTES,
    )


class Rows(NamedTuple):
    xb: jax.Array
    ssq: jax.Array


def _row_stats_body(x_ref, xb_ref, ssq_ref):
    x = x_ref[...]
    xb_ref[...] = x.astype(xb_ref.dtype)
    ssq_ref[...] = jnp.broadcast_to(jnp.sum(x * x, axis=-1, keepdims=True), ssq_ref.shape)


def row_stats(x, name):
    t, d = x.shape
    tr = _tile(t, NORM_TR)
    xb, ssq = pl.pallas_call(
        _row_stats_body,
        out_shape=(jax.ShapeDtypeStruct((t, d), BF16), jax.ShapeDtypeStruct((t, V7X_LANES), F32)),
        grid=(t // tr,),
        in_specs=[pl.BlockSpec((tr, d), lambda i: (i, 0))],
        out_specs=(pl.BlockSpec((tr, d), lambda i: (i, 0)),
                   pl.BlockSpec((tr, V7X_LANES), lambda i: (i, 0))),
        compiler_params=_params(1),
        name=name,
    )(x)
    return Rows(xb, ssq)


def _mm_body(*refs, normed, n_w, n_aux, epilogue):
    a_ref = refs[0]
    refs = refs[1:]
    if normed:
        ssq_ref, refs = refs[0], refs[1:]
    w_refs = refs[:n_w]
    aux_refs = refs[n_w:n_w + n_aux]
    out_refs = refs[n_w + n_aux:]
    a = a_ref[...]
    accs = [jnp.dot(a, w[...], preferred_element_type=F32) for w in w_refs]
    if normed:
        r = lax.rsqrt(ssq_ref[:, :1] * (1.0 / a.shape[1]) + EPS)
        accs = [acc * r for acc in accs]
    epilogue(accs, aux_refs, out_refs)


def matmul(a, ws, w_blk_offsets, n_cols, *, tm, tn, epilogue, out_shapes, out_specs,
           aux=(), aux_specs=(), w_lead=(), name):
    normed = isinstance(a, Rows)
    a_arr = a.xb if normed else a
    t, k = a_arr.shape
    tm = _tile(t, tm)
    tn = _tile(n_cols, tn)
    in_specs = [pl.BlockSpec((tm, k), lambda i, j: (i, 0))]
    args = [a_arr]
    if normed:
        in_specs.append(pl.BlockSpec((tm, V7X_LANES), lambda i, j: (i, 0)))
        args.append(a.ssq)
    lead_blk = (None,) * len(w_lead)
    for off in w_blk_offsets:
        in_specs.append(pl.BlockSpec(lead_blk + (k, tn),
                                     lambda i, j, off=off: tuple(w_lead) + (0, j + off)))
    in_specs += list(aux_specs)
    body = functools.partial(_mm_body, normed=normed, n_w=len(ws), n_aux=len(aux),
                             epilogue=epilogue)
    return pl.pallas_call(
        body,
        out_shape=out_shapes,
        grid=(t // tm, n_cols // tn),
        in_specs=in_specs,
        out_specs=out_specs,
        compiler_params=_params(2),
        name=name,
    )(*args, *ws, *aux)


def _ep_plain(accs, aux, outs):
    outs[0][...] = accs[0].astype(outs[0].dtype)


def _ep_swiglu(accs, aux, outs):
    g, u = accs
    outs[0][...] = (g * jax.nn.sigmoid(g) * u).astype(outs[0].dtype)


def _ep_residual(accs, aux, outs, *, alpha):
    x_out, xb_out, ssq_out = outs
    x_new = aux[0][...] + alpha * accs[0]
    x_out[...] = x_new
    xb_out[...] = x_new.astype(xb_out.dtype)

    @pl.when(pl.program_id(1) == 0)
    def _():
        ssq_out[...] = jnp.zeros_like(ssq_out)

    part = jnp.sum(x_new * x_new, axis=-1, keepdims=True)
    ssq_out[...] += jnp.broadcast_to(part, ssq_out.shape)


def _rope(t, cosf, sinf):
    return t * cosf + pltpu.roll(t, 64, 1) * sinf


def _ep_headnorm(accs, aux, outs, *, rope):
    acc = accs[0]
    if rope:
        cosf, sinf = aux[1][...], aux[2][...]
    for c in range(acc.shape[1] // V7X_LANES):
        sl = slice(c * V7X_LANES, (c + 1) * V7X_LANES)
        t = acc[:, sl]
        y = t * lax.rsqrt(jnp.mean(t * t, axis=-1, keepdims=True) + EPS) * aux[0][:, sl]
        if rope:
            y = _rope(y, cosf, sinf)
        outs[0][:, sl] = y.astype(outs[0].dtype)


def _ep_rownorm(accs, aux, outs):
    t = accs[0]
    y = t * lax.rsqrt(jnp.mean(t * t, axis=-1, keepdims=True) + EPS) * aux[0][...]
    outs[0][...] = y.astype(outs[0].dtype)


def _ep_mla_q(accs, aux, outs, *, n_valid):
    acc = accs[0]
    gain, cosf, sinf = aux[0][...], aux[1][...], aux[2][...]
    w = 2 * V7X_LANES
    for c in range(acc.shape[1] // w):
        t = acc[:, c * w:(c + 1) * w]
        r = lax.rsqrt(jnp.sum(t * t, axis=-1, keepdims=True) / n_valid + EPS)
        y = t * r * gain
        o = c * w
        outs[0][:, o:o + V7X_LANES] = y[:, :V7X_LANES].astype(outs[0].dtype)
        outs[0][:, o + V7X_LANES:o + w] = _rope(y[:, V7X_LANES:], cosf, sinf).astype(outs[0].dtype)


def _ep_mla_kv(accs, aux, outs, *, n_valid):
    acc = accs[0]
    kpe, gain_nope, gain_pe, cosf, sinf = (r[...] for r in aux)
    k_out, v_out = outs
    ssq_pe = jnp.sum(kpe * kpe, axis=-1, keepdims=True)
    pe_rot = _rope(kpe * gain_pe, cosf, sinf)
    w = 2 * V7X_LANES
    for c in range(acc.shape[1] // w):
        kn = acc[:, c * w:c * w + V7X_LANES]
        v = acc[:, c * w + V7X_LANES:(c + 1) * w]
        r = lax.rsqrt((jnp.sum(kn * kn, axis=-1, keepdims=True) + ssq_pe) / n_valid + EPS)
        o = c * w
        k_out[:, o:o + V7X_LANES] = (kn * r * gain_nope).astype(k_out.dtype)
        k_out[:, o + V7X_LANES:o + w] = (pe_rot * r).astype(k_out.dtype)
        v_out[:, o // 2:o // 2 + V7X_LANES] = v.astype(v_out.dtype)


def _out_tile(t, n, dtype, tm, tn):
    tm, tn = _tile(t, tm), _tile(n, tn)
    return jax.ShapeDtypeStruct((t, n), dtype), pl.BlockSpec((tm, tn), lambda i, j: (i, j))


def _col_blk(col_off, n, tn):
    tn = _tile(n, tn)
    assert col_off % tn == 0, (col_off, tn)
    return col_off // tn


def _n_rows(a):
    return (a.xb if isinstance(a, Rows) else a).shape[0]


def mm_plain(a, w, out_dtype, name, n_cols=None, col_off=0, tm=MM_TM, tn=MM_TN):
    t, n = _n_rows(a), n_cols or w.shape[1]
    shape, spec = _out_tile(t, n, out_dtype, tm, tn)
    return matmul(a, [w], [_col_blk(col_off, n, tn)], n, tm=tm, tn=tn, epilogue=_ep_plain,
                  out_shapes=shape, out_specs=spec, name=name)


def mm_swiglu(a, w_in, name, w_lead=(), tm=MM_TM, tn=MM_TN // 2):
    t, n = _n_rows(a), w_in.shape[-1] // 2
    tn = _tile(n, tn)
    shape, spec = _out_tile(t, n, BF16, tm, tn)
    return matmul(a, [w_in, w_in], [0, n // tn], n, tm=tm, tn=tn, epilogue=_ep_swiglu,
                  out_shapes=shape, out_specs=spec, w_lead=w_lead, name=name)


def mm_residual(a, w, res, alpha, name, w_lead=(), tm=MM_TM, tn=MM_TN // 2):
    t, n = _n_rows(a), w.shape[-1]
    tm_ = _tile(t, tm)
    shape, spec = _out_tile(t, n, F32, tm, tn)
    x, xb, ssq = matmul(
        a, [w], [0], n, tm=tm, tn=tn, epilogue=functools.partial(_ep_residual, alpha=alpha),
        out_shapes=(shape, jax.ShapeDtypeStruct((t, n), BF16),
                    jax.ShapeDtypeStruct((t, V7X_LANES), F32)),
        out_specs=(spec, spec, pl.BlockSpec((tm_, V7X_LANES), lambda i, j: (i, 0))),
        aux=[res], aux_specs=[spec], w_lead=w_lead, name=name)
    return x, Rows(xb, ssq)


def _rope_specs(t, seq, tm):
    tm = _tile(t, tm)
    assert seq % tm == 0
    nblk = seq // tm
    return pl.BlockSpec((tm, V7X_LANES), lambda i, j: (i % nblk, 0))


def mm_headnorm(a, w, gain_row, name, rope_tabs=None, seq=None, tm=MM_TM, tn=MM_TN):
    t, n = _n_rows(a), gain_row.shape[1]
    tn_ = _tile(n, tn)
    shape, spec = _out_tile(t, n, BF16, tm, tn)
    aux = [gain_row]
    aux_specs = [pl.BlockSpec((1, tn_), lambda i, j: (0, j))]
    if rope_tabs is not None:
        rs = _rope_specs(t, seq, tm)
        aux += list(rope_tabs)
        aux_specs += [rs, rs]
    return matmul(a, [w], [0], n, tm=tm, tn=tn,
                  epilogue=functools.partial(_ep_headnorm, rope=rope_tabs is not None),
                  out_shapes=shape, out_specs=spec, aux=aux, aux_specs=aux_specs,
                  name=name)


def mm_rownorm(a, w, gain_row, name, col_off=0, tm=MM_TM):
    t, n = _n_rows(a), gain_row.shape[1]
    shape, spec = _out_tile(t, n, BF16, tm, n)
    return matmul(a, [w], [_col_blk(col_off, n, n)], n, tm=tm, tn=n, epilogue=_ep_rownorm,
                  out_shapes=shape, out_specs=spec, aux=[gain_row],
                  aux_specs=[pl.BlockSpec((1, n), lambda i, j: (0, 0))], name=name)


def _gate_scan_body(x_ref, bias_ref, mult_ref, o_ref, carry_ref, *, blk):
    @pl.when(pl.program_id(1) == 0)
    def _():
        carry_ref[...] = jnp.zeros_like(carry_ref)

    y = x_ref[...] + bias_ref[...]
    c = jnp.minimum(y, 0.0) - jnp.log1p(jnp.exp(-jnp.abs(y)))
    row = lax.broadcasted_iota(jnp.int32, c.shape, 0)
    shift = 1
    while shift < blk:
        c = c + jnp.where(row >= shift, pltpu.roll(c, shift, 0), 0.0)
        shift *= 2
    c = c + carry_ref[...]
    carry_ref[...] = c[blk - 1:blk, :]
    mult = mult_ref[...]
    o_ref[...] = jnp.where(mult != 0.0, c * mult, y)


def gate_scan(x, bias_row, mult_row, batch, seq, name):
    blk = _tile(seq, SCAN_BLK)
    nblk = seq // blk
    spec = pl.BlockSpec((blk, V7X_LANES), lambda b, s: (b * nblk + s, 0))
    row = pl.BlockSpec((1, V7X_LANES), lambda b, s: (0, 0))
    return pl.pallas_call(
        functools.partial(_gate_scan_body, blk=blk),
        out_shape=jax.ShapeDtypeStruct(x.shape, F32),
        grid=(batch, nblk),
        in_specs=[spec, row, row],
        out_specs=spec,
        scratch_shapes=[pltpu.VMEM((1, V7X_LANES), F32)],
        compiler_params=_params(2),
        name=name,
    )(x, bias_row, mult_row)


def _pad_lanes(v, width=V7X_LANES):
    return jnp.pad(v, [(0, 0)] * (v.ndim - 1) + [(0, width - v.shape[-1])])


class _Stream:
    def __init__(self, q_cols, k_cols, head):
        self.q_cols, self.k_cols, self.head = q_cols, k_cols, head


def _flash_body(*refs, streams, hp, dv, tq, tk, seq, frame_causal, decay, n_extra, finalize):
    q_ref, k_ref, v_ref = refs[:3]
    pos = 3
    if decay:
        cq_ref, ck_ref = refs[3:5]
        pos = 5
    extra = refs[pos:pos + n_extra]
    o_ref = refs[pos + n_extra]
    vt_ref = refs[pos + n_extra + 1]
    ckt_ref = refs[pos + n_extra + 2] if decay else None
    qi = pl.program_id(2)

    @pl.when(qi == 0)
    def _():
        def fill(c, _):
            start = pl.multiple_of(c * tq, tq)
            for e in range(hp):
                vt_ref[e, :, pl.ds(start, tq)] = v_ref[pl.ds(start, tq), e * dv:(e + 1) * dv].T
                if decay:
                    ck = jnp.broadcast_to(ck_ref[e, :, pl.ds(start, tq)], (V7X_LANES, tq))
                    ckt_ref[e, pl.ds(start, tq), :] = ck.T
            return 0
        lax.fori_loop(0, seq // tq, fill, 0)

    def diag_mask(d):
        key = lax.broadcasted_iota(jnp.int32, (tk, tq), 0) + d * tk
        qry = lax.broadcasted_iota(jnp.int32, (tk, tq), 1)
        if frame_causal:
            return key <= qry
        shift = CHUNK.bit_length() - 1
        return jnp.right_shift(key, shift) <= jnp.right_shift(qry, shift)

    qs = [q_ref[:, st.q_cols] for st in streams]
    cqs = [cq_ref[st.head] if decay else None for st in streams]

    def block(j, state, mask):
        start = pl.multiple_of(j * tk, tk)
        ts = [lax.dot_general(k_ref[pl.ds(start, tk), st.k_cols], q, (((1,), (1,)), ((), ())),
                              preferred_element_type=F32) for st, q in zip(streams, qs)]
        out = []
        for st, t, cq, (m, l, acc) in zip(streams, ts, cqs, state):
            vt = vt_ref[st.head, :, pl.ds(start, tk)]
            if decay:
                ckb = ckt_ref[st.head, pl.ds(start, tk), :]
                t = t - jnp.concatenate([ckb] * (tq // V7X_LANES), axis=1)
            if mask is not None:
                t = jnp.where(mask, t, -jnp.inf)
            cmax = jnp.max(t, axis=0, keepdims=True)
            if decay:
                m_new = jnp.maximum(m, cmax + cq)
                mt = m_new - cq
            else:
                m_new = jnp.maximum(m, cmax)
                mt = m_new
            alpha = jnp.exp2(m - m_new)
            p = jnp.exp2(t - mt)
            l = alpha * l + jnp.sum(p, axis=0, keepdims=True)
            acc = alpha * acc + jnp.dot(vt, p.astype(BF16), preferred_element_type=F32)
            out.append((m_new, l, acc))
        return tuple(out)

    init = tuple((jnp.full((1, tq), -jnp.inf, F32), jnp.zeros((1, tq), F32),
                  jnp.zeros((dv, tq), F32)) for _ in streams)
    n_diag = tq // tk
    n_full = qi * n_diag
    state = lax.fori_loop(0, n_full, lambda j, s: block(j, s, None), init)
    for d in range(n_diag):
        state = block(n_full + d, state, diag_mask(d))
    finalize([acc / l for _, l, acc in state], extra, o_ref)


def _fin_heads(outs, extra, o_ref):
    dv = outs[0].shape[0]
    for e, o in enumerate(outs):
        o_ref[:, e * dv:(e + 1) * dv] = o.T.astype(o_ref.dtype)


def _fin_diff(outs, extra, o_ref, *, lam_init):
    lp = extra[0][...]
    sub_gain = extra[1][...]
    lam = (jnp.exp(jnp.sum(lp[0:1] * lp[1:2], axis=-1, keepdims=True))
           - jnp.exp(jnp.sum(lp[2:3] * lp[3:4], axis=-1, keepdims=True)) + lam_init)
    dv = outs[0].shape[0]
    for e in range(len(outs) // 2):
        y = (outs[2 * e] - lam * outs[2 * e + 1]).T
        y = y * lax.rsqrt(jnp.mean(y * y, axis=-1, keepdims=True) + EPS) * sub_gain
        o_ref[:, e * dv:(e + 1) * dv] = (y * (1.0 - lam_init)).astype(o_ref.dtype)


def flash_attention(q_arr, k_arr, v_arr, *, batch, seq, heads, hp, n_maps, dk, dv,
                    q_blk_off, k_blk_off, v_blk_off, frame_causal,
                    decay_rows=None, extra=(), extra_specs=(), finalize=_fin_heads, name):
    tq = _tile(seq, ATTN_TQ)
    tk = _tile(tq, ATTN_TK)
    assert tk % CHUNK == 0 and tk % V7X_LANES == 0 and heads % hp == 0
    nq = seq // tq
    qw = hp * n_maps * dk
    vw = hp * dv
    decay = decay_rows is not None
    streams = []
    for e in range(hp):
        for g in range(n_maps):
            c = (e * n_maps + g) * dk
            streams.append(_Stream(slice(c, c + dk), slice(c, c + dk), e))
    in_specs = [
        pl.BlockSpec((tq, qw), lambda b, h, i: (b * nq + i, h + q_blk_off)),
        pl.BlockSpec((seq, qw), lambda b, h, i: (b, h + k_blk_off)),
        pl.BlockSpec((seq, vw), lambda b, h, i: (b, h + v_blk_off)),
    ]
    args = [q_arr, k_arr, v_arr]
    scratch = [pltpu.VMEM((hp, dv, seq), BF16)]
    if decay:
        in_specs += [pl.BlockSpec((None, hp, 1, tq), lambda b, h, i: (b, h, 0, i)),
                     pl.BlockSpec((None, hp, 1, seq), lambda b, h, i: (b, h, 0, 0))]
        args += [decay_rows, decay_rows]
        scratch.append(pltpu.VMEM((hp, seq, V7X_LANES), F32))
    in_specs += list(extra_specs)
    args += list(extra)
    body = functools.partial(_flash_body, streams=streams, hp=hp, dv=dv, tq=tq, tk=tk, seq=seq,
                             frame_causal=frame_causal, decay=decay,
                             n_extra=len(extra), finalize=finalize)
    return pl.pallas_call(
        body,
        out_shape=jax.ShapeDtypeStruct((batch * seq, heads * dv), BF16),
        grid=(batch, heads // hp, nq),
        in_specs=in_specs,
        out_specs=pl.BlockSpec((tq, vw), lambda b, h, i: (b * nq + i, h)),
        scratch_shapes=scratch,
        compiler_params=_params(3),
        name=name,
    )(*args)


def _mlstm_body(q_ref, k_ref, v_ref, o_ref, ipc_ref, cumc_ref, gr_ref, gain_ref, y_ref,
                c_ref, n_ref, m_ref, cprev_ref, *, lc, heads, dk):
    h = pl.program_id(1)

    @pl.when(pl.program_id(2) == 0)
    def _():
        c_ref[...] = jnp.zeros_like(c_ref)
        n_ref[...] = jnp.zeros_like(n_ref)
        m_ref[...] = jnp.zeros_like(m_ref)
        cprev_ref[...] = jnp.zeros_like(cprev_ref)

    qscale = dk ** -0.5
    q = q_ref[...]
    k = k_ref[...]
    v = v_ref[...]
    ip_col = ipc_ref[...]
    ip_row = gr_ref[pl.ds(h, 1), :]
    cum_row = gr_ref[pl.ds(heads + h, 1), :]
    cprev = cprev_ref[...]
    m_prev = m_ref[...]
    b_col = cumc_ref[...] - cprev
    b_row = cum_row - cprev
    b_last = b_row[:, lc - 1:lc]

    row = lax.broadcasted_iota(jnp.int32, (lc, lc), 0)
    col = lax.broadcasted_iota(jnp.int32, (lc, lc), 1)
    d = jnp.where(col <= row, b_col - b_row + ip_row, -jnp.inf)
    inter = b_col + m_prev
    m_row = jnp.maximum(jnp.max(d, axis=-1, keepdims=True), inter)
    w_intra = jnp.exp(d - m_row)
    w_inter = jnp.exp(inter - m_row)

    qk = lax.dot_general(q, k, (((1,), (1,)), ((), ())), preferred_element_type=F32) * qscale
    a = w_intra * qk
    c_old = c_ref[...]
    num = (jnp.dot(a.astype(BF16), v, preferred_element_type=F32)
           + w_inter * (jnp.dot(q, c_old.astype(BF16), preferred_element_type=F32) * qscale))
    qn = jnp.sum(q.astype(F32) * n_ref[...], axis=-1, keepdims=True) * qscale
    den = jnp.sum(a, axis=-1, keepdims=True) + w_inter * qn
    hc = num / jnp.maximum(jnp.abs(den), jnp.exp(-m_row))

    g_row = b_last - b_row + ip_row
    g_col = b_last - b_col + ip_col
    m_new = jnp.maximum(b_last + m_prev, jnp.max(g_row, axis=-1, keepdims=True))
    decay = jnp.exp(b_last + m_prev - m_new)
    kw = k.astype(F32) * jnp.exp(g_col - m_new)
    c_ref[...] = decay * c_old + jnp.dot(kw.T.astype(BF16), v, preferred_element_type=F32)
    n_ref[...] = decay * n_ref[...] + jnp.sum(kw, axis=0, keepdims=True)
    m_ref[...] = m_new
    cprev_ref[...] = cum_row[:, lc - 1:lc]

    y = hc * lax.rsqrt(jnp.mean(hc * hc, axis=-1, keepdims=True) + EPS) * gain_ref[...]
    y_ref[...] = (y * jax.nn.sigmoid(o_ref[...])).astype(y_ref.dtype)


def mlstm_chunks(qkv, o, gate_cols, gate_rows, out_gain, *, batch, seq, heads, dk, dv, name):
    lc = _tile(seq, MLSTM_LC)
    nc = seq // lc
    kb, vb = heads, (2 * heads * dk) // dv
    col_spec = lambda off: pl.BlockSpec((None, None, lc, 1), lambda b, h, c: (b, h + off, c, 0))
    in_specs = [
        pl.BlockSpec((lc, dk), lambda b, h, c: (b * nc + c, h)),
        pl.BlockSpec((lc, dk), lambda b, h, c: (b * nc + c, kb + h)),
        pl.BlockSpec((lc, dv), lambda b, h, c: (b * nc + c, vb + h)),
        pl.BlockSpec((lc, dv), lambda b, h, c: (b * nc + c, h)),
        col_spec(0),
        col_spec(heads),
        pl.BlockSpec((None, 2 * heads, lc), lambda b, h, c: (b, 0, c)),
        pl.BlockSpec((None, 1, dv), lambda b, h, c: (h, 0, 0)),
    ]
    return pl.pallas_call(
        functools.partial(_mlstm_body, lc=lc, heads=heads, dk=dk),
        out_shape=jax.ShapeDtypeStruct((batch * seq, heads * dv), BF16),
        grid=(batch, heads, nc),
        in_specs=in_specs,
        out_specs=pl.BlockSpec((lc, dv), lambda b, h, c: (b * nc + c, h)),
        scratch_shapes=[pltpu.VMEM((dk, dv), F32), pltpu.VMEM((1, dk), F32),
                        pltpu.VMEM((1, 1), F32), pltpu.VMEM((1, 1), F32)],
        compiler_params=_params(3),
        name=name,
    )(qkv, qkv, qkv, o, gate_cols, gate_cols, gate_rows, out_gain.reshape(heads, 1, dv))


def _rope_tables(seq, dim):
    pos = jnp.arange(seq, dtype=F32)
    inv = ROPE_THETA ** (-jnp.arange(0, dim, 2, dtype=F32) / dim)
    ang = pos[:, None] * inv[None, :]
    return jnp.cos(ang), jnp.sin(ang)


def _ffn(x, h, w_in_all, w_out_all, lead, name):
    act = mm_swiglu(h, w_in_all, name + "_in", w_lead=lead)
    return mm_residual(act, w_out_all, x, 0.5, name + "_out", w_lead=lead)


def _gate_arrays(gates, n_rows, batch, seq):
    rows = gates.reshape(batch, seq, V7X_LANES)[:, :, :n_rows].transpose(0, 2, 1)
    return rows[..., None], rows


def _fox(x, h, w_in, q_gain, k_gain, f_bias, w_out, batch, seq, name):
    d = x.shape[1]
    hd = d // FOX_HEADS
    hp = 4
    q_scale = hd ** -0.5 * LOG2E
    gain_row = jnp.concatenate([jnp.tile(q_gain * q_scale, FOX_HEADS),
                                jnp.tile(k_gain, FOX_HEADS)])[None, :]
    wb = w_in.astype(BF16)
    qk = mm_headnorm(h, wb, gain_row.astype(F32), name + "_qk")
    v = mm_plain(h, wb, BF16, name + "_v", n_cols=d, col_off=2 * d)
    f = mm_plain(h, _pad_lanes(w_in[:, 3 * d:]).astype(BF16), F32, name + "_f", tn=V7X_LANES)
    mult = jnp.where(jnp.arange(V7X_LANES) < FOX_HEADS, LOG2E, 0.0).astype(F32)[None, :]
    cum = gate_scan(f, _pad_lanes(f_bias[None, :]), mult, batch, seq, name + "_scan")
    _, cum_rows = _gate_arrays(cum, FOX_HEADS, batch, seq)
    o = flash_attention(qk, qk, v, batch=batch, seq=seq, heads=FOX_HEADS, hp=hp, n_maps=1,
                        dk=hd, dv=hd, q_blk_off=0, k_blk_off=FOX_HEADS // hp, v_blk_off=0,
                        frame_causal=True, decay_rows=cum_rows[:, :, None, :],
                        name=name + "_attn")
    return mm_residual(o, w_out.astype(BF16), x, 1.0, name + "_out")


def _diff(x, h, w_in, q_gain, k_gain, lam_p, sub_gain, w_out, lam_init, batch, seq, name):
    d = x.shape[1]
    hd = d // (2 * DIFF_HEADS)
    dv = 2 * hd
    n_maps = 2 * DIFF_HEADS
    cos, sin = _rope_tables(seq, hd)
    tabs = (jnp.concatenate([cos, cos], -1), jnp.concatenate([-sin, sin], -1))
    q_scale = hd ** -0.5 * LOG2E
    gain_row = jnp.concatenate([jnp.tile(q_gain * q_scale, n_maps), jnp.tile(k_gain, n_maps)])[None, :]
    wb = w_in.astype(BF16)
    qk = mm_headnorm(h, wb, gain_row.astype(F32), name + "_qk", rope_tabs=tabs, seq=seq)
    v = mm_plain(h, wb, BF16, name + "_v", n_cols=DIFF_HEADS * dv, col_off=2 * d)
    const = lambda shape: pl.BlockSpec(shape, lambda b, hh, i: (0,) * len(shape))
    hp = 2
    o = flash_attention(qk, qk, v, batch=batch, seq=seq, heads=DIFF_HEADS, hp=hp, n_maps=2,
                        dk=hd, dv=dv, q_blk_off=0, k_blk_off=DIFF_HEADS // hp, v_blk_off=0,
                        frame_causal=False, extra=[lam_p.astype(F32), sub_gain[None, :].astype(F32)],
                        extra_specs=[const((4, hd)), const((1, dv))],
                        finalize=functools.partial(_fin_diff, lam_init=lam_init),
                        name=name + "_attn")
    return mm_residual(o, w_out.astype(BF16), x, 1.0, name + "_out")


def _mla_rope_layout(t):
    half = MLA_ROPE_DIM // 2
    z = jnp.zeros(t.shape[:-1] + (V7X_LANES // 2 - half,), t.dtype)
    return jnp.concatenate([t[..., :half], z, t[..., half:], z], axis=-1)


def _mla(x, h, w_in, q_lat_gain, w_q_up, kv_lat_gain, w_kv_up, q_gain, k_gain, w_out,
         batch, seq, name):
    nh, dn, dr, dv = MLA_HEADS, MLA_NOPE_DIM, MLA_ROPE_DIM, MLA_V_DIM
    q_rank, kv_rank = w_q_up.shape[0], w_kv_up.shape[0]
    cos, sin = _rope_tables(seq, dr)
    z = jnp.zeros_like(cos)
    tabs = (jnp.concatenate([cos, z, cos, z], -1), jnp.concatenate([-sin, z, sin, z], -1))

    wb = w_in.astype(BF16)
    cq = mm_rownorm(h, wb, q_lat_gain[None, :].astype(F32), name + "_cq")
    ckv = mm_rownorm(h, wb, kv_lat_gain[None, :].astype(F32), name + "_ckv", col_off=q_rank)
    kpe = mm_plain(h, _mla_rope_layout(w_in[:, q_rank + kv_rank:]).astype(BF16), F32,
                   name + "_kpe", tn=V7X_LANES)

    def pad_head(t):
        return jnp.concatenate([t[..., :dn], _mla_rope_layout(t[..., dn:])], axis=-1)

    wq = pad_head(w_q_up.reshape(q_rank, nh, dn + dr)).reshape(q_rank, nh * 2 * V7X_LANES)
    t = x.shape[0]
    tm = _tile(t, MM_TM)
    rs = _rope_specs(t, seq, tm)
    qshape, qspec = _out_tile(t, wq.shape[1], BF16, tm, MM_TN)
    row256 = pl.BlockSpec((1, 2 * V7X_LANES), lambda i, j: (0, 0))
    row128 = pl.BlockSpec((1, V7X_LANES), lambda i, j: (0, 0))
    q = matmul(cq, [wq.astype(BF16)], [0], wq.shape[1], tm=tm, tn=MM_TN,
               epilogue=functools.partial(_ep_mla_q, n_valid=dn + dr),
               out_shapes=qshape, out_specs=qspec,
               aux=[pad_head(q_gain * ((dn + dr) ** -0.5 * LOG2E))[None, :].astype(F32),
                    tabs[0], tabs[1]],
               aux_specs=[row256, rs, rs], name=name + "_q")

    n_kv = w_kv_up.shape[1]
    tn = _tile(n_kv, MM_TN)
    kshape, kspec = _out_tile(t, n_kv, BF16, tm, tn)
    vshape = jax.ShapeDtypeStruct((t, n_kv // 2), BF16)
    vspec = pl.BlockSpec((tm, tn // 2), lambda i, j: (i, j))
    k, v = matmul(ckv, [w_kv_up.astype(BF16)], [0], n_kv, tm=tm, tn=tn,
                  epilogue=functools.partial(_ep_mla_kv, n_valid=dn + dr),
                  out_shapes=(kshape, vshape), out_specs=(kspec, vspec),
                  aux=[kpe, k_gain[None, :dn].astype(F32),
                       _mla_rope_layout(k_gain[dn:])[None, :].astype(F32), tabs[0], tabs[1]],
                  aux_specs=[pl.BlockSpec((tm, V7X_LANES), lambda i, j: (i, 0)), row128, row128, rs, rs],
                  name=name + "_kv")
    o = flash_attention(q, k, v, batch=batch, seq=seq, heads=nh, hp=4, n_maps=1,
                        dk=2 * V7X_LANES, dv=dv, q_blk_off=0, k_blk_off=0, v_blk_off=0,
                        frame_causal=False, name=name + "_attn")
    return mm_residual(o, w_out.astype(BF16), x, 1.0, name + "_out")


def _mlstm(x, h, w_in, gate_bias, out_gain, w_out, batch, seq, name):
    d = x.shape[1]
    nh = MLSTM_HEADS
    dk, dv = d // (2 * nh), d // nh
    n_qkv = 2 * nh * dk + nh * dv
    wb = w_in.astype(BF16)
    qkv = mm_plain(h, wb, BF16, name + "_qkv", n_cols=n_qkv)
    o = mm_plain(h, wb, F32, name + "_o", n_cols=d, col_off=n_qkv)
    gates = mm_plain(h, _pad_lanes(w_in[:, n_qkv + d:]).astype(BF16), F32, name + "_g", tn=V7X_LANES)
    bias = _pad_lanes(jnp.concatenate([gate_bias[0], gate_bias[1]])[None, :])
    lane = jnp.arange(V7X_LANES)
    mult = ((lane >= nh) & (lane < 2 * nh)).astype(F32)[None, :]
    scanned = gate_scan(gates, bias, mult, batch, seq, name + "_scan")
    gate_cols, gate_rows = _gate_arrays(scanned, 2 * nh, batch, seq)
    y = mlstm_chunks(qkv, o, gate_cols, gate_rows, out_gain.astype(F32), batch=batch, seq=seq,
                     heads=nh, dk=dk, dv=dv, name=name + "_cell")
    return mm_residual(y, w_out.astype(BF16), x, 1.0, name + "_out")


def kernel(x, norm_ffn, ffn_w_in, ffn_w_out, norm_mix, fox_w_in, fox_q_gain, fox_k_gain, fox_f_bias, fox_w_out, diff_w_in, diff_q_gain, diff_k_gain, diff_lambda, diff_sub_gain, diff_w_out, mla_w_in, mla_q_lat_gain, mla_w_q_up, mla_kv_lat_gain, mla_w_kv_up, mla_q_gain, mla_k_gain, mla_w_out, mlstm_w_in, mlstm_gate_bias, mlstm_out_gain, mlstm_w_out):
    batch, seq, d = x.shape
    depth = norm_mix.shape[0]
    x = x.reshape(batch * seq, d)
    ffn_in = (norm_ffn[..., None] * ffn_w_in).astype(BF16)
    ffn_out = ffn_w_out.astype(BF16)
    h = row_stats(x, "rows0")
    for i in range(depth):
        kind, occ = i % N_MIXERS, i // N_MIXERS
        gain = norm_mix[i][:, None]
        x, h = _ffn(x, h, ffn_in, ffn_out, (i, 0), f"l{i}_ffa")
        if kind == 0:
            x, h = _fox(x, h, gain * fox_w_in[occ], fox_q_gain[occ], fox_k_gain[occ],
                        fox_f_bias[occ], fox_w_out[occ], batch, seq, f"l{i}_fox")
        elif kind == 1:
            lam_init = 0.8 - 0.6 * math.exp(-0.3 * i)
            x, h = _diff(x, h, gain * diff_w_in[occ], diff_q_gain[occ], diff_k_gain[occ],
                         diff_lambda[occ], diff_sub_gain[occ], diff_w_out[occ], lam_init,
                         batch, seq, f"l{i}_diff")
        elif kind == 2:
            x, h = _mla(x, h, gain * mla_w_in[occ], mla_q_lat_gain[occ], mla_w_q_up[occ],
                        mla_kv_lat_gain[occ], mla_w_kv_up[occ], mla_q_gain[occ], mla_k_gain[occ],
                        mla_w_out[occ], batch, seq, f"l{i}_mla")
        else:
            x, h = _mlstm(x, h, gain * mlstm_w_in[occ], mlstm_gate_bias[occ], mlstm_out_gain[occ],
                          mlstm_w_out[occ], batch, seq, f"l{i}_mlstm")
        x, h = _ffn(x, h, ffn_in, ffn_out, (i, 1), f"l{i}_ffb")
    return x.reshape(batch, seq, d)
```

```python
import functools
import math
from typing import NamedTuple

import jax
import jax.numpy as jnp
from jax import lax
from jax.experimental import pallas as pl
from jax.experimental.pallas import tpu as pltpu

F32 = jnp.float32
BF16 = jnp.bfloat16

CHUNK = 64
EPS = 1e-6
ROPE_THETA = 10000.0
LOG2E = math.log2(math.e)
N_MIXERS = 4
FOX_HEADS = 32
DIFF_HEADS = 16
MLA_HEADS = 32
MLA_NOPE_DIM = 128
MLA_ROPE_DIM = 64
MLA_V_DIM = 128
MLSTM_HEADS = 8

V7X_LANES = 128
V7X_VMEM_LIMIT_BYTES = 56 * 1024 * 1024

MM_TM = 1024
MM_TN = 1024
MM_ROW_CHUNK = 256
ATTN_TQ = 512
ATTN_TK = 512
SUM_ROWS = 16
MLSTM_LC = 256
NORM_TR = 256
SCAN_BLK = 512


def _tile(n, pref):
    t = min(n, pref)
    assert n % t == 0, (n, pref)
    return t


def _params(n_grid):
    return pltpu.CompilerParams(
        dimension_semantics=("arbitrary",) * n_grid,
        vmem_limit_bytes=V7X_VMEM_LIMIT_BYTES,
    )


class Rows(NamedTuple):
    xb: jax.Array
    ssq: jax.Array


def _row_stats_body(x_ref, xb_ref, ssq_ref):
    x = x_ref[...]
    xb_ref[...] = x.astype(xb_ref.dtype)
    sq = x * x
    part = sq[:, :V7X_LANES]
    for c in range(1, sq.shape[1] // V7X_LANES):
        part = part + sq[:, c * V7X_LANES:(c + 1) * V7X_LANES]
    ssq_ref[...] = part


def row_stats(x, name):
    t, d = x.shape
    tr = _tile(t, NORM_TR)
    xb, ssq = pl.pallas_call(
        _row_stats_body,
        out_shape=(jax.ShapeDtypeStruct((t, d), BF16), jax.ShapeDtypeStruct((t, V7X_LANES), F32)),
        grid=(t // tr,),
        in_specs=[pl.BlockSpec((tr, d), lambda i: (i, 0))],
        out_specs=(pl.BlockSpec((tr, d), lambda i: (i, 0)),
                   pl.BlockSpec((tr, V7X_LANES), lambda i: (i, 0))),
        compiler_params=_params(1),
        name=name,
    )(x)
    return Rows(xb, ssq)


def _mm_body(*refs, normed, n_w, n_aux, epilogue, row_chunk):
    a_ref = refs[0]
    refs = refs[1:]
    if normed:
        ssq_ref, refs = refs[0], refs[1:]
    w_refs = refs[:n_w]
    aux_refs = refs[n_w:n_w + n_aux]
    out_refs = refs[n_w + n_aux:]
    tm = a_ref.shape[0]
    for r0 in range(0, tm, row_chunk):
        rows = slice(r0, r0 + row_chunk)
        view = lambda ref: ref.at[rows] if ref.shape[0] == tm else ref
        a = a_ref[rows, :]
        accs = [jnp.dot(a, w[...], preferred_element_type=F32) for w in w_refs]
        if normed:
            ssq = jnp.sum(ssq_ref[rows, :], axis=-1, keepdims=True)
            r = lax.rsqrt(ssq * (1.0 / a.shape[1]) + EPS)
            accs = [acc * r for acc in accs]
        epilogue(accs, [view(x) for x in aux_refs], [view(x) for x in out_refs])


def matmul(a, ws, w_blk_offsets, n_cols, *, tm, tn, epilogue, out_shapes, out_specs,
           aux=(), aux_specs=(), w_lead=(), row_chunk=None, name):
    normed = isinstance(a, Rows)
    a_arr = a.xb if normed else a
    t, k = a_arr.shape
    tm = _tile(t, tm)
    tn = _tile(n_cols, tn)
    in_specs = [pl.BlockSpec((tm, k), lambda i, j: (i, 0))]
    args = [a_arr]
    if normed:
        in_specs.append(pl.BlockSpec((tm, V7X_LANES), lambda i, j: (i, 0)))
        args.append(a.ssq)
    lead_blk = (None,) * len(w_lead)
    for off in w_blk_offsets:
        in_specs.append(pl.BlockSpec(lead_blk + (k, tn),
                                     lambda i, j, off=off: tuple(w_lead) + (0, j + off)))
    in_specs += list(aux_specs)
    body = functools.partial(_mm_body, normed=normed, n_w=len(ws), n_aux=len(aux),
                             epilogue=epilogue, row_chunk=_tile(tm, row_chunk or tm))
    return pl.pallas_call(
        body,
        out_shape=out_shapes,
        grid=(t // tm, n_cols // tn),
        in_specs=in_specs,
        out_specs=out_specs,
        compiler_params=_params(2),
        name=name,
    )(*args, *ws, *aux)


def _ep_plain(accs, aux, outs):
    outs[0][...] = accs[0].astype(outs[0].dtype)


def _ep_swiglu(accs, aux, outs):
    g, u = accs
    outs[0][...] = (g * jax.nn.sigmoid(g) * u).astype(outs[0].dtype)


def _ep_residual(accs, aux, outs, *, alpha):
    x_out, xb_out, ssq_out = outs
    x_new = aux[0][...] + alpha * accs[0]
    x_out[...] = x_new
    xb_out[...] = x_new.astype(xb_out.dtype)

    @pl.when(pl.program_id(1) == 0)
    def _():
        ssq_out[...] = jnp.zeros_like(ssq_out)

    sq = x_new * x_new
    part = sq[:, :V7X_LANES]
    for c in range(1, sq.shape[1] // V7X_LANES):
        part = part + sq[:, c * V7X_LANES:(c + 1) * V7X_LANES]
    ssq_out[...] += part


def _rope(t, cosf, sinf):
    return t * cosf + pltpu.roll(t, 64, 1) * sinf


def _ep_headnorm(accs, aux, outs, *, rope):
    acc = accs[0]
    if rope:
        cosf, sinf = aux[1][...], aux[2][...]
    for c in range(acc.shape[1] // V7X_LANES):
        sl = slice(c * V7X_LANES, (c + 1) * V7X_LANES)
        t = acc[:, sl]
        y = t * lax.rsqrt(jnp.mean(t * t, axis=-1, keepdims=True) + EPS) * aux[0][:, sl]
        if rope:
            y = _rope(y, cosf, sinf)
        outs[0][:, sl] = y.astype(outs[0].dtype)


def _ep_rownorm(accs, aux, outs):
    t = accs[0]
    y = t * lax.rsqrt(jnp.mean(t * t, axis=-1, keepdims=True) + EPS) * aux[0][...]
    outs[0][...] = y.astype(outs[0].dtype)


def _ep_mla_q(accs, aux, outs, *, n_valid):
    acc = accs[0]
    gain, cosf, sinf = aux[0][...], aux[1][...], aux[2][...]
    w = 2 * V7X_LANES
    for c in range(acc.shape[1] // w):
        t = acc[:, c * w:(c + 1) * w]
        r = lax.rsqrt(jnp.sum(t * t, axis=-1, keepdims=True) / n_valid + EPS)
        y = t * r * gain
        o = c * w
        outs[0][:, o:o + V7X_LANES] = y[:, :V7X_LANES].astype(outs[0].dtype)
        outs[0][:, o + V7X_LANES:o + w] = _rope(y[:, V7X_LANES:], cosf, sinf).astype(outs[0].dtype)


def _ep_mla_kv(accs, aux, outs, *, n_valid):
    acc = accs[0]
    kpe, gain_nope, gain_pe, cosf, sinf = (r[...] for r in aux)
    k_out, v_out = outs
    ssq_pe = jnp.sum(kpe * kpe, axis=-1, keepdims=True)
    pe_rot = _rope(kpe * gain_pe, cosf, sinf)
    w = 2 * V7X_LANES
    for c in range(acc.shape[1] // w):
        kn = acc[:, c * w:c * w + V7X_LANES]
        v = acc[:, c * w + V7X_LANES:(c + 1) * w]
        r = lax.rsqrt((jnp.sum(kn * kn, axis=-1, keepdims=True) + ssq_pe) / n_valid + EPS)
        o = c * w
        k_out[:, o:o + V7X_LANES] = (kn * r * gain_nope).astype(k_out.dtype)
        k_out[:, o + V7X_LANES:o + w] = (pe_rot * r).astype(k_out.dtype)
        v_out[:, o // 2:o // 2 + V7X_LANES] = v.astype(v_out.dtype)


def _out_tile(t, n, dtype, tm, tn):
    tm, tn = _tile(t, tm), _tile(n, tn)
    return jax.ShapeDtypeStruct((t, n), dtype), pl.BlockSpec((tm, tn), lambda i, j: (i, j))


def _col_blk(col_off, n, tn):
    tn = _tile(n, tn)
    assert col_off % tn == 0, (col_off, tn)
    return col_off // tn


def _n_rows(a):
    return (a.xb if isinstance(a, Rows) else a).shape[0]


def mm_plain(a, w, out_dtype, name, n_cols=None, col_off=0, tm=MM_TM, tn=MM_TN):
    t, n = _n_rows(a), n_cols or w.shape[1]
    shape, spec = _out_tile(t, n, out_dtype, tm, tn)
    return matmul(a, [w], [_col_blk(col_off, n, tn)], n, tm=tm, tn=tn, epilogue=_ep_plain,
                  out_shapes=shape, out_specs=spec, name=name)


def mm_swiglu(a, w_in, name, w_lead=(), tm=MM_TM, tn=MM_TN // 2):
    t, n = _n_rows(a), w_in.shape[-1] // 2
    tn = _tile(n, tn)
    shape, spec = _out_tile(t, n, BF16, tm, tn)
    return matmul(a, [w_in, w_in], [0, n // tn], n, tm=tm, tn=tn, epilogue=_ep_swiglu,
                  out_shapes=shape, out_specs=spec, w_lead=w_lead, name=name)


def mm_residual(a, w, res, alpha, name, w_lead=(), tm=MM_TM, tn=MM_TN // 2):
    t, n = _n_rows(a), w.shape[-1]
    tm_ = _tile(t, tm)
    shape, spec = _out_tile(t, n, F32, tm, tn)
    x, xb, ssq = matmul(
        a, [w], [0], n, tm=tm, tn=tn, epilogue=functools.partial(_ep_residual, alpha=alpha),
        out_shapes=(shape, jax.ShapeDtypeStruct((t, n), BF16),
                    jax.ShapeDtypeStruct((t, V7X_LANES), F32)),
        out_specs=(spec, spec, pl.BlockSpec((tm_, V7X_LANES), lambda i, j: (i, 0))),
        aux=[res], aux_specs=[spec], w_lead=w_lead, name=name)
    return x, Rows(xb, ssq)


def _rope_specs(t, seq, tm):
    tm = _tile(t, tm)
    assert seq % tm == 0
    nblk = seq // tm
    return pl.BlockSpec((tm, V7X_LANES), lambda i, j: (i % nblk, 0))


def mm_headnorm(a, w, gain_row, name, rope_tabs=None, seq=None, tm=MM_TM, tn=MM_TN):
    t, n = _n_rows(a), gain_row.shape[1]
    tn_ = _tile(n, tn)
    shape, spec = _out_tile(t, n, BF16, tm, tn)
    aux = [gain_row]
    aux_specs = [pl.BlockSpec((1, tn_), lambda i, j: (0, j))]
    if rope_tabs is not None:
        rs = _rope_specs(t, seq, tm)
        aux += list(rope_tabs)
        aux_specs += [rs, rs]
    return matmul(a, [w], [0], n, tm=tm, tn=tn,
                  epilogue=functools.partial(_ep_headnorm, rope=rope_tabs is not None),
                  out_shapes=shape, out_specs=spec, aux=aux, aux_specs=aux_specs,
                  row_chunk=MM_ROW_CHUNK, name=name)


def mm_rownorm(a, w, gain_row, name, col_off=0, tm=MM_TM):
    t, n = _n_rows(a), gain_row.shape[1]
    shape, spec = _out_tile(t, n, BF16, tm, n)
    return matmul(a, [w], [_col_blk(col_off, n, n)], n, tm=tm, tn=n, epilogue=_ep_rownorm,
                  out_shapes=shape, out_specs=spec, aux=[gain_row],
                  aux_specs=[pl.BlockSpec((1, n), lambda i, j: (0, 0))], name=name)


def _gate_scan_body(x_ref, bias_ref, mult_ref, o_ref, carry_ref, *, blk):
    @pl.when(pl.program_id(1) == 0)
    def _():
        carry_ref[...] = jnp.zeros_like(carry_ref)

    y = x_ref[...] + bias_ref[...]
    c = jnp.minimum(y, 0.0) - jnp.log1p(jnp.exp(-jnp.abs(y)))
    row = lax.broadcasted_iota(jnp.int32, c.shape, 0)
    shift = 1
    while shift < blk:
        c = c + jnp.where(row >= shift, pltpu.roll(c, shift, 0), 0.0)
        shift *= 2
    c = c + carry_ref[...]
    carry_ref[...] = c[blk - 1:blk, :]
    mult = mult_ref[...]
    o_ref[...] = jnp.where(mult != 0.0, c * mult, y)


def gate_scan(x, bias_row, mult_row, batch, seq, name):
    blk = _tile(seq, SCAN_BLK)
    nblk = seq // blk
    spec = pl.BlockSpec((blk, V7X_LANES), lambda b, s: (b * nblk + s, 0))
    row = pl.BlockSpec((1, V7X_LANES), lambda b, s: (0, 0))
    return pl.pallas_call(
        functools.partial(_gate_scan_body, blk=blk),
        out_shape=jax.ShapeDtypeStruct(x.shape, F32),
        grid=(batch, nblk),
        in_specs=[spec, row, row],
        out_specs=spec,
        scratch_shapes=[pltpu.VMEM((1, V7X_LANES), F32)],
        compiler_params=_params(2),
        name=name,
    )(x, bias_row, mult_row)


def _pad_lanes(v, width=V7X_LANES):
    return jnp.pad(v, [(0, 0)] * (v.ndim - 1) + [(0, width - v.shape[-1])])


class _Stream:
    def __init__(self, q_cols, k_cols, head):
        self.q_cols, self.k_cols, self.head = q_cols, k_cols, head


def _flash_body(*refs, streams, hp, dv, tq, tk, seq, frame_causal, decay, n_extra, finalize):
    q_ref, k_ref, v_ref = refs[:3]
    pos = 3
    if decay:
        cq_ref, ck_ref = refs[3:5]
        pos = 5
    extra = refs[pos:pos + n_extra]
    o_ref = refs[pos + n_extra]
    vt_ref = refs[pos + n_extra + 1]
    ckt_ref = refs[pos + n_extra + 2] if decay else None
    qi = pl.program_id(2)

    @pl.when(qi == 0)
    def _():
        def fill(c, _):
            start = pl.multiple_of(c * tq, tq)
            ones_rows = (lax.broadcasted_iota(jnp.int32, (SUM_ROWS, tq), 0) == 0).astype(vt_ref.dtype)
            for e in range(hp):
                vt_ref[e, :dv, pl.ds(start, tq)] = v_ref[pl.ds(start, tq), e * dv:(e + 1) * dv].T
                vt_ref[e, dv:, pl.ds(start, tq)] = ones_rows
                if decay:
                    ck = jnp.broadcast_to(ck_ref[e, :, pl.ds(start, tq)], (V7X_LANES, tq))
                    ckt_ref[e, pl.ds(start, tq), :] = ck.T
            return 0
        lax.fori_loop(0, seq // tq, fill, 0)

    def diag_mask(d):
        key = lax.broadcasted_iota(jnp.int32, (tk, tq), 0) + d * tk
        qry = lax.broadcasted_iota(jnp.int32, (tk, tq), 1)
        if frame_causal:
            return key <= qry
        shift = CHUNK.bit_length() - 1
        return jnp.right_shift(key, shift) <= jnp.right_shift(qry, shift)

    qs = [q_ref[:, st.q_cols] for st in streams]
    cqs = [cq_ref[st.head] if decay else None for st in streams]

    def block(j, state, mask):
        start = pl.multiple_of(j * tk, tk)
        ts = [lax.dot_general(k_ref[pl.ds(start, tk), st.k_cols], q, (((1,), (1,)), ((), ())),
                              preferred_element_type=F32) for st, q in zip(streams, qs)]
        out = []
        for st, t, cq, (m, acc) in zip(streams, ts, cqs, state):
            vt = vt_ref[st.head, :, pl.ds(start, tk)]
            if decay:
                ckb = ckt_ref[st.head, pl.ds(start, tk), :]
                t = t - jnp.concatenate([ckb] * (tq // V7X_LANES), axis=1)
            if mask is not None:
                t = jnp.where(mask, t, -jnp.inf)
            cmax = jnp.max(t, axis=0, keepdims=True)
            if decay:
                m_new = jnp.maximum(m, cmax + cq)
                mt = m_new - cq
            else:
                m_new = jnp.maximum(m, cmax)
                mt = m_new
            alpha = jnp.exp2(m - m_new)
            p = jnp.exp2(t - mt)
            acc = alpha * acc + jnp.dot(vt, p.astype(BF16), preferred_element_type=F32)
            out.append((m_new, acc))
        return tuple(out)

    init = tuple((jnp.full((1, tq), -jnp.inf, F32), jnp.zeros((dv + SUM_ROWS, tq), F32))
                 for _ in streams)
    n_diag = tq // tk
    n_full = qi * n_diag
    state = lax.fori_loop(0, n_full, lambda j, s: block(j, s, None), init)
    for d in range(n_diag):
        state = block(n_full + d, state, diag_mask(d))
    finalize([acc[:dv] / acc[dv:dv + 1] for _, acc in state], extra, o_ref)


def _fin_heads(outs, extra, o_ref):
    dv = outs[0].shape[0]
    for e, o in enumerate(outs):
        o_ref[:, e * dv:(e + 1) * dv] = o.T.astype(o_ref.dtype)


def _fin_diff(outs, extra, o_ref, *, lam_init):
    lp = extra[0][...]
    sub_gain = extra[1][...]
    lam = (jnp.exp(jnp.sum(lp[0:1] * lp[1:2], axis=-1, keepdims=True))
           - jnp.exp(jnp.sum(lp[2:3] * lp[3:4], axis=-1, keepdims=True)) + lam_init)
    dv = outs[0].shape[0]
    for e in range(len(outs) // 2):
        y = (outs[2 * e] - lam * outs[2 * e + 1]).T
        y = y * lax.rsqrt(jnp.mean(y * y, axis=-1, keepdims=True) + EPS) * sub_gain
        o_ref[:, e * dv:(e + 1) * dv] = (y * (1.0 - lam_init)).astype(o_ref.dtype)


def flash_attention(q_arr, k_arr, v_arr, *, batch, seq, heads, hp, n_maps, dk, dv,
                    q_blk_off, k_blk_off, v_blk_off, frame_causal,
                    decay_rows=None, extra=(), extra_specs=(), finalize=_fin_heads, name):
    tq = _tile(seq, ATTN_TQ)
    tk = _tile(tq, ATTN_TK)
    assert tk % CHUNK == 0 and tk % V7X_LANES == 0 and heads % hp == 0
    nq = seq // tq
    qw = hp * n_maps * dk
    vw = hp * dv
    decay = decay_rows is not None
    streams = []
    for e in range(hp):
        for g in range(n_maps):
            c = (e * n_maps + g) * dk
            streams.append(_Stream(slice(c, c + dk), slice(c, c + dk), e))
    in_specs = [
        pl.BlockSpec((tq, qw), lambda b, h, i: (b * nq + i, h + q_blk_off)),
        pl.BlockSpec((seq, qw), lambda b, h, i: (b, h + k_blk_off)),
        pl.BlockSpec((seq, vw), lambda b, h, i: (b, h + v_blk_off)),
    ]
    args = [q_arr, k_arr, v_arr]
    scratch = [pltpu.VMEM((hp, dv + SUM_ROWS, seq), BF16)]
    if decay:
        in_specs += [pl.BlockSpec((None, hp, 1, tq), lambda b, h, i: (b, h, 0, i)),
                     pl.BlockSpec((None, hp, 1, seq), lambda b, h, i: (b, h, 0, 0))]
        args += [decay_rows, decay_rows]
        scratch.append(pltpu.VMEM((hp, seq, V7X_LANES), F32))
    in_specs += list(extra_specs)
    args += list(extra)
    body = functools.partial(_flash_body, streams=streams, hp=hp, dv=dv, tq=tq, tk=tk, seq=seq,
                             frame_causal=frame_causal, decay=decay,
                             n_extra=len(extra), finalize=finalize)
    return pl.pallas_call(
        body,
        out_shape=jax.ShapeDtypeStruct((batch * seq, heads * dv), BF16),
        grid=(batch, heads // hp, nq),
        in_specs=in_specs,
        out_specs=pl.BlockSpec((tq, vw), lambda b, h, i: (b * nq + i, h)),
        scratch_shapes=scratch,
        compiler_params=_params(3),
        name=name,
    )(*args)


def _mlstm_body(q_ref, k_ref, v_ref, o_ref, ipc_ref, cumc_ref, gr_ref, gain_ref, y_ref,
                c_ref, n_ref, m_ref, cprev_ref, *, lc, heads, dk):
    h = pl.program_id(1)

    @pl.when(pl.program_id(2) == 0)
    def _():
        c_ref[...] = jnp.zeros_like(c_ref)
        n_ref[...] = jnp.zeros_like(n_ref)
        m_ref[...] = jnp.zeros_like(m_ref)
        cprev_ref[...] = jnp.zeros_like(cprev_ref)

    qscale = dk ** -0.5
    q = q_ref[...]
    k = k_ref[...]
    v = v_ref[...]
    ip_col = ipc_ref[...]
    ip_row = gr_ref[pl.ds(h, 1), :]
    cum_row = gr_ref[pl.ds(heads + h, 1), :]
    cprev = cprev_ref[...]
    m_prev = m_ref[...]
    b_col = cumc_ref[...] - cprev
    b_row = cum_row - cprev
    b_last = b_row[:, lc - 1:lc]

    row = lax.broadcasted_iota(jnp.int32, (lc, lc), 0)
    col = lax.broadcasted_iota(jnp.int32, (lc, lc), 1)
    d = jnp.where(col <= row, b_col - b_row + ip_row, -jnp.inf)
    inter = b_col + m_prev
    m_row = jnp.maximum(jnp.max(d, axis=-1, keepdims=True), inter)
    w_intra = jnp.exp(d - m_row)
    w_inter = jnp.exp(inter - m_row)

    qk = lax.dot_general(q, k, (((1,), (1,)), ((), ())), preferred_element_type=F32) * qscale
    a = w_intra * qk
    c_old = c_ref[...]
    num = (jnp.dot(a.astype(BF16), v, preferred_element_type=F32)
           + w_inter * (jnp.dot(q, c_old.astype(BF16), preferred_element_type=F32) * qscale))
    qn = jnp.sum(q.astype(F32) * n_ref[...], axis=-1, keepdims=True) * qscale
    den = jnp.sum(a, axis=-1, keepdims=True) + w_inter * qn
    hc = num / jnp.maximum(jnp.abs(den), jnp.exp(-m_row))

    g_row = b_last - b_row + ip_row
    g_col = b_last - b_col + ip_col
    m_new = jnp.maximum(b_last + m_prev, jnp.max(g_row, axis=-1, keepdims=True))
    decay = jnp.exp(b_last + m_prev - m_new)
    kw = k.astype(F32) * jnp.exp(g_col - m_new)
    c_ref[...] = decay * c_old + jnp.dot(kw.T.astype(BF16), v, preferred_element_type=F32)
    n_ref[...] = decay * n_ref[...] + jnp.sum(kw, axis=0, keepdims=True)
    m_ref[...] = m_new
    cprev_ref[...] = cum_row[:, lc - 1:lc]

    y = hc * lax.rsqrt(jnp.mean(hc * hc, axis=-1, keepdims=True) + EPS) * gain_ref[...]
    y_ref[...] = (y * jax.nn.sigmoid(o_ref[...])).astype(y_ref.dtype)


def mlstm_chunks(qkv, o, gate_cols, gate_rows, out_gain, *, batch, seq, heads, dk, dv, name):
    lc = _tile(seq, MLSTM_LC)
    nc = seq // lc
    kb, vb = heads, (2 * heads * dk) // dv
    col_spec = lambda off: pl.BlockSpec((None, None, lc, 1), lambda b, h, c: (b, h + off, c, 0))
    in_specs = [
        pl.BlockSpec((lc, dk), lambda b, h, c: (b * nc + c, h)),
        pl.BlockSpec((lc, dk), lambda b, h, c: (b * nc + c, kb + h)),
        pl.BlockSpec((lc, dv), lambda b, h, c: (b * nc + c, vb + h)),
        pl.BlockSpec((lc, dv), lambda b, h, c: (b * nc + c, h)),
        col_spec(0),
        col_spec(heads),
        pl.BlockSpec((None, 2 * heads, lc), lambda b, h, c: (b, 0, c)),
        pl.BlockSpec((None, 1, dv), lambda b, h, c: (h, 0, 0)),
    ]
    return pl.pallas_call(
        functools.partial(_mlstm_body, lc=lc, heads=heads, dk=dk),
        out_shape=jax.ShapeDtypeStruct((batch * seq, heads * dv), BF16),
        grid=(batch, heads, nc),
        in_specs=in_specs,
        out_specs=pl.BlockSpec((lc, dv), lambda b, h, c: (b * nc + c, h)),
        scratch_shapes=[pltpu.VMEM((dk, dv), F32), pltpu.VMEM((1, dk), F32),
                        pltpu.VMEM((1, 1), F32), pltpu.VMEM((1, 1), F32)],
        compiler_params=_params(3),
        name=name,
    )(qkv, qkv, qkv, o, gate_cols, gate_cols, gate_rows, out_gain.reshape(heads, 1, dv))


def _rope_tables(seq, dim):
    pos = jnp.arange(seq, dtype=F32)
    inv = ROPE_THETA ** (-jnp.arange(0, dim, 2, dtype=F32) / dim)
    ang = pos[:, None] * inv[None, :]
    return jnp.cos(ang), jnp.sin(ang)


def _ffn(x, h, w_in_all, w_out_all, lead, name):
    act = mm_swiglu(h, w_in_all, name + "_in", w_lead=lead)
    return mm_residual(act, w_out_all, x, 0.5, name + "_out", w_lead=lead)


def _gate_arrays(gates, n_rows, batch, seq):
    rows = gates.reshape(batch, seq, V7X_LANES)[:, :, :n_rows].transpose(0, 2, 1)
    return rows[..., None], rows


def _fox(x, h, w_in, q_gain, k_gain, f_bias, w_out, batch, seq, name):
    d = x.shape[1]
    hd = d // FOX_HEADS
    hp = 4
    q_scale = hd ** -0.5 * LOG2E
    gain_row = jnp.concatenate([jnp.tile(q_gain * q_scale, FOX_HEADS),
                                jnp.tile(k_gain, FOX_HEADS)])[None, :]
    wb = w_in.astype(BF16)
    qk = mm_headnorm(h, wb, gain_row.astype(F32), name + "_qk")
    v = mm_plain(h, wb, BF16, name + "_v", n_cols=d, col_off=2 * d)
    f = mm_plain(h, _pad_lanes(w_in[:, 3 * d:]).astype(BF16), F32, name + "_f", tn=V7X_LANES)
    mult = jnp.where(jnp.arange(V7X_LANES) < FOX_HEADS, LOG2E, 0.0).astype(F32)[None, :]
    cum = gate_scan(f, _pad_lanes(f_bias[None, :]), mult, batch, seq, name + "_scan")
    _, cum_rows = _gate_arrays(cum, FOX_HEADS, batch, seq)
    o = flash_attention(qk, qk, v, batch=batch, seq=seq, heads=FOX_HEADS, hp=hp, n_maps=1,
                        dk=hd, dv=hd, q_blk_off=0, k_blk_off=FOX_HEADS // hp, v_blk_off=0,
                        frame_causal=True, decay_rows=cum_rows[:, :, None, :],
                        name=name + "_attn")
    return mm_residual(o, w_out.astype(BF16), x, 1.0, name + "_out")


def _diff(x, h, w_in, q_gain, k_gain, lam_p, sub_gain, w_out, lam_init, batch, seq, name):
    d = x.shape[1]
    hd = d // (2 * DIFF_HEADS)
    dv = 2 * hd
    n_maps = 2 * DIFF_HEADS
    cos, sin = _rope_tables(seq, hd)
    tabs = (jnp.concatenate([cos, cos], -1), jnp.concatenate([-sin, sin], -1))
    q_scale = hd ** -0.5 * LOG2E
    gain_row = jnp.concatenate([jnp.tile(q_gain * q_scale, n_maps), jnp.tile(k_gain, n_maps)])[None, :]
    wb = w_in.astype(BF16)
    qk = mm_headnorm(h, wb, gain_row.astype(F32), name + "_qk", rope_tabs=tabs, seq=seq)
    v = mm_plain(h, wb, BF16, name + "_v", n_cols=DIFF_HEADS * dv, col_off=2 * d)
    const = lambda shape: pl.BlockSpec(shape, lambda b, hh, i: (0,) * len(shape))
    hp = 2
    o = flash_attention(qk, qk, v, batch=batch, seq=seq, heads=DIFF_HEADS, hp=hp, n_maps=2,
                        dk=hd, dv=dv, q_blk_off=0, k_blk_off=DIFF_HEADS // hp, v_blk_off=0,
                        frame_causal=False, extra=[lam_p.astype(F32), sub_gain[None, :].astype(F32)],
                        extra_specs=[const((4, hd)), const((1, dv))],
                        finalize=functools.partial(_fin_diff, lam_init=lam_init),
                        name=name + "_attn")
    return mm_residual(o, w_out.astype(BF16), x, 1.0, name + "_out")


def _mla_rope_layout(t):
    half = MLA_ROPE_DIM // 2
    z = jnp.zeros(t.shape[:-1] + (V7X_LANES // 2 - half,), t.dtype)
    return jnp.concatenate([t[..., :half], z, t[..., half:], z], axis=-1)


def _mla(x, h, w_in, q_lat_gain, w_q_up, kv_lat_gain, w_kv_up, q_gain, k_gain, w_out,
         batch, seq, name):
    nh, dn, dr, dv = MLA_HEADS, MLA_NOPE_DIM, MLA_ROPE_DIM, MLA_V_DIM
    q_rank, kv_rank = w_q_up.shape[0], w_kv_up.shape[0]
    cos, sin = _rope_tables(seq, dr)
    z = jnp.zeros_like(cos)
    tabs = (jnp.concatenate([cos, z, cos, z], -1), jnp.concatenate([-sin, z, sin, z], -1))

    wb = w_in.astype(BF16)
    cq = mm_rownorm(h, wb, q_lat_gain[None, :].astype(F32), name + "_cq")
    ckv = mm_rownorm(h, wb, kv_lat_gain[None, :].astype(F32), name + "_ckv", col_off=q_rank)
    kpe = mm_plain(h, _mla_rope_layout(w_in[:, q_rank + kv_rank:]).astype(BF16), F32,
                   name + "_kpe", tn=V7X_LANES)

    def pad_head(t):
        return jnp.concatenate([t[..., :dn], _mla_rope_layout(t[..., dn:])], axis=-1)

    wq = pad_head(w_q_up.reshape(q_rank, nh, dn + dr)).reshape(q_rank, nh * 2 * V7X_LANES)
    t = x.shape[0]
    tm = _tile(t, MM_TM)
    rs = _rope_specs(t, seq, tm)
    qshape, qspec = _out_tile(t, wq.shape[1], BF16, tm, MM_TN)
    row256 = pl.BlockSpec((1, 2 * V7X_LANES), lambda i, j: (0, 0))
    row128 = pl.BlockSpec((1, V7X_LANES), lambda i, j: (0, 0))
    q = matmul(cq, [wq.astype(BF16)], [0], wq.shape[1], tm=tm, tn=MM_TN,
               epilogue=functools.partial(_ep_mla_q, n_valid=dn + dr),
               out_shapes=qshape, out_specs=qspec,
               aux=[pad_head(q_gain * ((dn + dr) ** -0.5 * LOG2E))[None, :].astype(F32),
                    tabs[0], tabs[1]],
               aux_specs=[row256, rs, rs], row_chunk=MM_ROW_CHUNK, name=name + "_q")

    n_kv = w_kv_up.shape[1]
    tn = _tile(n_kv, MM_TN)
    kshape, kspec = _out_tile(t, n_kv, BF16, tm, tn)
    vshape = jax.ShapeDtypeStruct((t, n_kv // 2), BF16)
    vspec = pl.BlockSpec((tm, tn // 2), lambda i, j: (i, j))
    k, v = matmul(ckv, [w_kv_up.astype(BF16)], [0], n_kv, tm=tm, tn=tn,
                  epilogue=functools.partial(_ep_mla_kv, n_valid=dn + dr),
                  out_shapes=(kshape, vshape), out_specs=(kspec, vspec),
                  aux=[kpe, k_gain[None, :dn].astype(F32),
                       _mla_rope_layout(k_gain[dn:])[None, :].astype(F32), tabs[0], tabs[1]],
                  aux_specs=[pl.BlockSpec((tm, V7X_LANES), lambda i, j: (i, 0)), row128, row128, rs, rs],
                  row_chunk=2 * MM_ROW_CHUNK, name=name + "_kv")
    o = flash_attention(q, k, v, batch=batch, seq=seq, heads=nh, hp=4, n_maps=1,
                        dk=2 * V7X_LANES, dv=dv, q_blk_off=0, k_blk_off=0, v_blk_off=0,
                        frame_causal=False, name=name + "_attn")
    return mm_residual(o, w_out.astype(BF16), x, 1.0, name + "_out")


def _mlstm(x, h, w_in, gate_bias, out_gain, w_out, batch, seq, name):
    d = x.shape[1]
    nh = MLSTM_HEADS
    dk, dv = d // (2 * nh), d // nh
    n_qkv = 2 * nh * dk + nh * dv
    wb = w_in.astype(BF16)
    qkv = mm_plain(h, wb, BF16, name + "_qkv", n_cols=n_qkv)
    o = mm_plain(h, wb, F32, name + "_o", n_cols=d, col_off=n_qkv)
    gates = mm_plain(h, _pad_lanes(w_in[:, n_qkv + d:]).astype(BF16), F32, name + "_g", tn=V7X_LANES)
    bias = _pad_lanes(jnp.concatenate([gate_bias[0], gate_bias[1]])[None, :])
    lane = jnp.arange(V7X_LANES)
    mult = ((lane >= nh) & (lane < 2 * nh)).astype(F32)[None, :]
    scanned = gate_scan(gates, bias, mult, batch, seq, name + "_scan")
    gate_cols, gate_rows = _gate_arrays(scanned, 2 * nh, batch, seq)
    y = mlstm_chunks(qkv, o, gate_cols, gate_rows, out_gain.astype(F32), batch=batch, seq=seq,
                     heads=nh, dk=dk, dv=dv, name=name + "_cell")
    return mm_residual(y, w_out.astype(BF16), x, 1.0, name + "_out")


def kernel(x, norm_ffn, ffn_w_in, ffn_w_out, norm_mix, fox_w_in, fox_q_gain, fox_k_gain, fox_f_bias, fox_w_out, diff_w_in, diff_q_gain, diff_k_gain, diff_lambda, diff_sub_gain, diff_w_out, mla_w_in, mla_q_lat_gain, mla_w_q_up, mla_kv_lat_gain, mla_w_kv_up, mla_q_gain, mla_k_gain, mla_w_out, mlstm_w_in, mlstm_gate_bias, mlstm_out_gain, mlstm_w_out):
    batch, seq, d = x.shape
    depth = norm_mix.shape[0]
    x = x.reshape(batch * seq, d)
    ffn_in = (norm_ffn[..., None] * ffn_w_in).astype(BF16)
    ffn_out = ffn_w_out.astype(BF16)
    h = row_stats(x, "rows0")
    for i in range(depth):
        kind, occ = i % N_MIXERS, i // N_MIXERS
        gain = norm_mix[i][:, None]
        x, h = _ffn(x, h, ffn_in, ffn_out, (i, 0), f"l{i}_ffa")
        if kind == 0:
            x, h = _fox(x, h, gain * fox_w_in[occ], fox_q_gain[occ], fox_k_gain[occ],
                        fox_f_bias[occ], fox_w_out[occ], batch, seq, f"l{i}_fox")
        elif kind == 1:
            lam_init = 0.8 - 0.6 * math.exp(-0.3 * i)
            x, h = _diff(x, h, gain * diff_w_in[occ], diff_q_gain[occ], diff_k_gain[occ],
                         diff_lambda[occ], diff_sub_gain[occ], diff_w_out[occ], lam_init,
                         batch, seq, f"l{i}_diff")
        elif kind == 2:
            x, h = _mla(x, h, gain * mla_w_in[occ], mla_q_lat_gain[occ], mla_w_q_up[occ],
                        mla_kv_lat_gain[occ], mla_w_kv_up[occ], mla_q_gain[occ], mla_k_gain[occ],
                        mla_w_out[occ], batch, seq, f"l{i}_mla")
        else:
            x, h = _mlstm(x, h, gain * mlstm_w_in[occ], mlstm_gate_bias[occ], mlstm_out_gain[occ],
                          mlstm_w_out[occ], batch, seq, f"l{i}_mlstm")
        x, h = _ffn(x, h, ffn_in, ffn_out, (i, 1), f"l{i}_ffb")
    return x.reshape(batch, seq, d)
```

```python
import functools
import math
from typing import NamedTuple

import jax
import jax.numpy as jnp
from jax import lax
from jax.experimental import pallas as pl
from jax.experimental.pallas import tpu as pltpu

F32 = jnp.float32
BF16 = jnp.bfloat16

CHUNK = 64
EPS = 1e-6
ROPE_THETA = 10000.0
LOG2E = math.log2(math.e)
N_MIXERS = 4
FOX_HEADS = 32
DIFF_HEADS = 16
MLA_HEADS = 32
MLA_NOPE_DIM = 128
MLA_ROPE_DIM = 64
MLA_V_DIM = 128
MLSTM_HEADS = 8

V7X_LANES = 128
V7X_VMEM_LIMIT_BYTES = 56 * 1024 * 1024

MM_TM = 1024
MM_TN = 1024
MM_ROW_CHUNK = 256
ATTN_TQ = 512
ATTN_TK = 512
SUM_ROWS = 16
MLSTM_LC = 256
NORM_TR = 256
SCAN_BLK = 512


def _tile(n, pref):
    t = min(n, pref)
    assert n % t == 0, (n, pref)
    return t


def _params(n_grid):
    return pltpu.CompilerParams(
        dimension_semantics=("arbitrary",) * n_grid,
        vmem_limit_bytes=V7X_VMEM_LIMIT_BYTES,
    )


class Rows(NamedTuple):
    xb: jax.Array
    ssq: jax.Array


def _row_stats_body(x_ref, xb_ref, ssq_ref):
    x = x_ref[...]
    xb_ref[...] = x.astype(xb_ref.dtype)
    sq = x * x
    part = sq[:, :V7X_LANES]
    for c in range(1, sq.shape[1] // V7X_LANES):
        part = part + sq[:, c * V7X_LANES:(c + 1) * V7X_LANES]
    ssq_ref[...] = part


def row_stats(x, name):
    t, d = x.shape
    tr = _tile(t, NORM_TR)
    xb, ssq = pl.pallas_call(
        _row_stats_body,
        out_shape=(jax.ShapeDtypeStruct((t, d), BF16), jax.ShapeDtypeStruct((t, V7X_LANES), F32)),
        grid=(t // tr,),
        in_specs=[pl.BlockSpec((tr, d), lambda i: (i, 0))],
        out_specs=(pl.BlockSpec((tr, d), lambda i: (i, 0)),
                   pl.BlockSpec((tr, V7X_LANES), lambda i: (i, 0))),
        compiler_params=_params(1),
        name=name,
    )(x)
    return Rows(xb, ssq)


def _mm_body(*refs, normed, n_w, n_aux, epilogue, row_chunk):
    a_ref = refs[0]
    refs = refs[1:]
    if normed:
        ssq_ref, refs = refs[0], refs[1:]
    w_refs = refs[:n_w]
    aux_refs = refs[n_w:n_w + n_aux]
    out_refs = refs[n_w + n_aux:]
    tm = a_ref.shape[0]
    for r0 in range(0, tm, row_chunk):
        rows = slice(r0, r0 + row_chunk)
        view = lambda ref: ref.at[rows] if ref.shape[0] == tm else ref
        a = a_ref[rows, :]
        accs = [jnp.dot(a, w[...], preferred_element_type=F32) for w in w_refs]
        if normed:
            ssq = jnp.sum(ssq_ref[rows, :], axis=-1, keepdims=True)
            r = lax.rsqrt(ssq * (1.0 / a.shape[1]) + EPS)
            accs = [acc * r for acc in accs]
        epilogue(accs, [view(x) for x in aux_refs], [view(x) for x in out_refs])


def matmul(a, ws, w_blk_offsets, n_cols, *, tm, tn, epilogue, out_shapes, out_specs,
           aux=(), aux_specs=(), w_lead=(), row_chunk=None, name):
    normed = isinstance(a, Rows)
    a_arr = a.xb if normed else a
    t, k = a_arr.shape
    tm = _tile(t, tm)
    tn = _tile(n_cols, tn)
    in_specs = [pl.BlockSpec((tm, k), lambda i, j: (i, 0))]
    args = [a_arr]
    if normed:
        in_specs.append(pl.BlockSpec((tm, V7X_LANES), lambda i, j: (i, 0)))
        args.append(a.ssq)
    lead_blk = (None,) * len(w_lead)
    for off in w_blk_offsets:
        in_specs.append(pl.BlockSpec(lead_blk + (k, tn),
                                     lambda i, j, off=off: tuple(w_lead) + (0, j + off)))
    in_specs += list(aux_specs)
    body = functools.partial(_mm_body, normed=normed, n_w=len(ws), n_aux=len(aux),
                             epilogue=epilogue, row_chunk=_tile(tm, row_chunk or tm))
    return pl.pallas_call(
        body,
        out_shape=out_shapes,
        grid=(t // tm, n_cols // tn),
        in_specs=in_specs,
        out_specs=out_specs,
        compiler_params=_params(2),
        name=name,
    )(*args, *ws, *aux)


def _ep_plain(accs, aux, outs):
    outs[0][...] = accs[0].astype(outs[0].dtype)


def _ep_swiglu(accs, aux, outs):
    g, u = accs
    outs[0][...] = (g * jax.nn.sigmoid(g) * u).astype(outs[0].dtype)


def _ep_residual(accs, aux, outs, *, alpha):
    x_out, xb_out, ssq_out = outs
    x_new = aux[0][...] + alpha * accs[0]
    x_out[...] = x_new
    xb_out[...] = x_new.astype(xb_out.dtype)

    @pl.when(pl.program_id(1) == 0)
    def _():
        ssq_out[...] = jnp.zeros_like(ssq_out)

    sq = x_new * x_new
    part = sq[:, :V7X_LANES]
    for c in range(1, sq.shape[1] // V7X_LANES):
        part = part + sq[:, c * V7X_LANES:(c + 1) * V7X_LANES]
    ssq_out[...] += part


def _rope(t, cosf, sinf):
    return t * cosf + pltpu.roll(t, V7X_LANES // 2, 1) * sinf


def _ep_headnorm(accs, aux, outs, *, rope):
    acc = accs[0]
    if rope:
        cosf, sinf = aux[1][...], aux[2][...]
    for c in range(acc.shape[1] // V7X_LANES):
        sl = slice(c * V7X_LANES, (c + 1) * V7X_LANES)
        t = acc[:, sl]
        y = t * lax.rsqrt(jnp.mean(t * t, axis=-1, keepdims=True) + EPS) * aux[0][:, sl]
        if rope:
            y = _rope(y, cosf, sinf)
        outs[0][:, sl] = y.astype(outs[0].dtype)


def _ep_rownorm(accs, aux, outs):
    t = accs[0]
    y = t * lax.rsqrt(jnp.mean(t * t, axis=-1, keepdims=True) + EPS) * aux[0][...]
    outs[0][...] = y.astype(outs[0].dtype)


def _ep_mla_q(accs, aux, outs, *, n_valid):
    acc = accs[0]
    gain, cosf, sinf = aux[0][...], aux[1][...], aux[2][...]
    w = 2 * V7X_LANES
    for c in range(acc.shape[1] // w):
        t = acc[:, c * w:(c + 1) * w]
        r = lax.rsqrt(jnp.sum(t * t, axis=-1, keepdims=True) / n_valid + EPS)
        y = t * r * gain
        o = c * w
        outs[0][:, o:o + V7X_LANES] = y[:, :V7X_LANES].astype(outs[0].dtype)
        outs[0][:, o + V7X_LANES:o + w] = _rope(y[:, V7X_LANES:], cosf, sinf).astype(outs[0].dtype)


def _ep_mla_kv(accs, aux, outs, *, n_valid):
    acc = accs[0]
    kpe, gain_nope, gain_pe, cosf, sinf = (r[...] for r in aux)
    k_out, v_out = outs
    ssq_pe = jnp.sum(kpe * kpe, axis=-1, keepdims=True)
    pe_rot = _rope(kpe * gain_pe, cosf, sinf)
    w = 2 * V7X_LANES
    for c in range(acc.shape[1] // w):
        kn = acc[:, c * w:c * w + V7X_LANES]
        v = acc[:, c * w + V7X_LANES:(c + 1) * w]
        r = lax.rsqrt((jnp.sum(kn * kn, axis=-1, keepdims=True) + ssq_pe) / n_valid + EPS)
        o = c * w
        k_out[:, o:o + V7X_LANES] = (kn * r * gain_nope).astype(k_out.dtype)
        k_out[:, o + V7X_LANES:o + w] = (pe_rot * r).astype(k_out.dtype)
        v_out[:, o // 2:o // 2 + V7X_LANES] = v.astype(v_out.dtype)


def _out_tile(t, n, dtype, tm, tn):
    tm, tn = _tile(t, tm), _tile(n, tn)
    return jax.ShapeDtypeStruct((t, n), dtype), pl.BlockSpec((tm, tn), lambda i, j: (i, j))


def _col_blk(col_off, n, tn):
    tn = _tile(n, tn)
    assert col_off % tn == 0, (col_off, tn)
    return col_off // tn


def _n_rows(a):
    return (a.xb if isinstance(a, Rows) else a).shape[0]


def mm_plain(a, w, out_dtype, name, n_cols=None, col_off=0, tm=MM_TM, tn=MM_TN):
    t, n = _n_rows(a), n_cols or w.shape[1]
    shape, spec = _out_tile(t, n, out_dtype, tm, tn)
    return matmul(a, [w], [_col_blk(col_off, n, tn)], n, tm=tm, tn=tn, epilogue=_ep_plain,
                  out_shapes=shape, out_specs=spec, name=name)


def mm_swiglu(a, w_in, name, w_lead=(), tm=MM_TM, tn=MM_TN // 2):
    t, n = _n_rows(a), w_in.shape[-1] // 2
    tn = _tile(n, tn)
    shape, spec = _out_tile(t, n, BF16, tm, tn)
    return matmul(a, [w_in, w_in], [0, n // tn], n, tm=tm, tn=tn, epilogue=_ep_swiglu,
                  out_shapes=shape, out_specs=spec, w_lead=w_lead, name=name)


def mm_residual(a, w, res, alpha, name, w_lead=(), tm=MM_TM, tn=MM_TN // 2):
    t, n = _n_rows(a), w.shape[-1]
    tm_ = _tile(t, tm)
    shape, spec = _out_tile(t, n, F32, tm, tn)
    x, xb, ssq = matmul(
        a, [w], [0], n, tm=tm, tn=tn, epilogue=functools.partial(_ep_residual, alpha=alpha),
        out_shapes=(shape, jax.ShapeDtypeStruct((t, n), BF16),
                    jax.ShapeDtypeStruct((t, V7X_LANES), F32)),
        out_specs=(spec, spec, pl.BlockSpec((tm_, V7X_LANES), lambda i, j: (i, 0))),
        aux=[res], aux_specs=[spec], w_lead=w_lead, name=name)
    return x, Rows(xb, ssq)


def _rope_specs(t, seq, tm):
    tm = _tile(t, tm)
    assert seq % tm == 0
    nblk = seq // tm
    return pl.BlockSpec((tm, V7X_LANES), lambda i, j: (i % nblk, 0))


def mm_headnorm(a, w, gain_row, name, rope_tabs=None, seq=None, tm=MM_TM, tn=MM_TN):
    t, n = _n_rows(a), gain_row.shape[1]
    tn_ = _tile(n, tn)
    shape, spec = _out_tile(t, n, BF16, tm, tn)
    aux = [gain_row]
    aux_specs = [pl.BlockSpec((1, tn_), lambda i, j: (0, j))]
    if rope_tabs is not None:
        rs = _rope_specs(t, seq, tm)
        aux += list(rope_tabs)
        aux_specs += [rs, rs]
    return matmul(a, [w], [0], n, tm=tm, tn=tn,
                  epilogue=functools.partial(_ep_headnorm, rope=rope_tabs is not None),
                  out_shapes=shape, out_specs=spec, aux=aux, aux_specs=aux_specs,
                  row_chunk=MM_ROW_CHUNK, name=name)


def mm_rownorm(a, w, gain_row, name, col_off=0, tm=MM_TM):
    t, n = _n_rows(a), gain_row.shape[1]
    shape, spec = _out_tile(t, n, BF16, tm, n)
    return matmul(a, [w], [_col_blk(col_off, n, n)], n, tm=tm, tn=n, epilogue=_ep_rownorm,
                  out_shapes=shape, out_specs=spec, aux=[gain_row],
                  aux_specs=[pl.BlockSpec((1, n), lambda i, j: (0, 0))], name=name)


def _gate_scan_body(x_ref, bias_ref, mult_ref, o_ref, carry_ref, *, blk):
    @pl.when(pl.program_id(1) == 0)
    def _():
        carry_ref[...] = jnp.zeros_like(carry_ref)

    y = x_ref[...] + bias_ref[...]
    c = jnp.minimum(y, 0.0) - jnp.log1p(jnp.exp(-jnp.abs(y)))
    row = lax.broadcasted_iota(jnp.int32, c.shape, 0)
    shift = 1
    while shift < blk:
        c = c + jnp.where(row >= shift, pltpu.roll(c, shift, 0), 0.0)
        shift *= 2
    c = c + carry_ref[...]
    carry_ref[...] = c[blk - 1:blk, :]
    mult = mult_ref[...]
    o_ref[...] = jnp.where(mult != 0.0, c * mult, y)


def gate_scan(x, bias_row, mult_row, batch, seq, name):
    blk = _tile(seq, SCAN_BLK)
    nblk = seq // blk
    spec = pl.BlockSpec((blk, V7X_LANES), lambda b, s: (b * nblk + s, 0))
    row = pl.BlockSpec((1, V7X_LANES), lambda b, s: (0, 0))
    return pl.pallas_call(
        functools.partial(_gate_scan_body, blk=blk),
        out_shape=jax.ShapeDtypeStruct(x.shape, F32),
        grid=(batch, nblk),
        in_specs=[spec, row, row],
        out_specs=spec,
        scratch_shapes=[pltpu.VMEM((1, V7X_LANES), F32)],
        compiler_params=_params(2),
        name=name,
    )(x, bias_row, mult_row)


def _pad_lanes(v, width=V7X_LANES):
    return jnp.pad(v, [(0, 0)] * (v.ndim - 1) + [(0, width - v.shape[-1])])


class _Stream:
    def __init__(self, q_cols, k_cols, head):
        self.q_cols, self.k_cols, self.head = q_cols, k_cols, head


def _flash_body(*refs, streams, hp, dv, tq, tk, seq, frame_causal, decay, n_extra, finalize):
    q_ref, k_ref, v_ref = refs[:3]
    pos = 3
    if decay:
        cq_ref, ck_ref = refs[3:5]
        pos = 5
    extra = refs[pos:pos + n_extra]
    o_ref = refs[pos + n_extra]
    vt_ref = refs[pos + n_extra + 1]
    ckt_ref = refs[pos + n_extra + 2] if decay else None
    qi = pl.program_id(2)

    @pl.when(qi == 0)
    def _():
        def fill(c, _):
            start = pl.multiple_of(c * tq, tq)
            ones_rows = (lax.broadcasted_iota(jnp.int32, (SUM_ROWS, tq), 0) == 0).astype(vt_ref.dtype)
            for e in range(hp):
                vt_ref[e, :dv, pl.ds(start, tq)] = v_ref[pl.ds(start, tq), e * dv:(e + 1) * dv].T
                vt_ref[e, dv:, pl.ds(start, tq)] = ones_rows
                if decay:
                    ck = jnp.broadcast_to(ck_ref[e, :, pl.ds(start, tq)], (V7X_LANES, tq))
                    ckt_ref[e, pl.ds(start, tq), :] = ck.T
            return 0
        lax.fori_loop(0, seq // tq, fill, 0)

    def diag_mask(d):
        key = lax.broadcasted_iota(jnp.int32, (tk, tq), 0) + d * tk
        qry = lax.broadcasted_iota(jnp.int32, (tk, tq), 1)
        if frame_causal:
            return key <= qry
        shift = CHUNK.bit_length() - 1
        return jnp.right_shift(key, shift) <= jnp.right_shift(qry, shift)

    qs = [q_ref[:, st.q_cols] for st in streams]
    cqs = [cq_ref[st.head] if decay else None for st in streams]

    def block(j, state, mask):
        start = pl.multiple_of(j * tk, tk)
        ts = [lax.dot_general(k_ref[pl.ds(start, tk), st.k_cols], q, (((1,), (1,)), ((), ())),
                              preferred_element_type=F32) for st, q in zip(streams, qs)]
        out = []
        for st, t, cq, (m, acc) in zip(streams, ts, cqs, state):
            vt = vt_ref[st.head, :, pl.ds(start, tk)]
            if decay:
                ckb = ckt_ref[st.head, pl.ds(start, tk), :]
                t = t - jnp.concatenate([ckb] * (tq // V7X_LANES), axis=1)
            if mask is not None:
                t = jnp.where(mask, t, -jnp.inf)
            cmax = jnp.max(t, axis=0, keepdims=True)
            if decay:
                m_new = jnp.maximum(m, cmax + cq)
                mt = m_new - cq
            else:
                m_new = jnp.maximum(m, cmax)
                mt = m_new
            alpha = jnp.exp2(m - m_new)
            p = jnp.exp2(t - mt)
            acc = alpha * acc + jnp.dot(vt, p.astype(BF16), preferred_element_type=F32)
            out.append((m_new, acc))
        return tuple(out)

    init = tuple((jnp.full((1, tq), -jnp.inf, F32), jnp.zeros((dv + SUM_ROWS, tq), F32))
                 for _ in streams)
    n_diag = tq // tk
    n_full = qi * n_diag
    state = lax.fori_loop(0, n_full, lambda j, s: block(j, s, None), init)
    for d in range(n_diag):
        state = block(n_full + d, state, diag_mask(d))
    finalize([acc[:dv] / acc[dv:dv + 1] for _, acc in state], extra, o_ref)


def _fin_heads(outs, extra, o_ref):
    dv = outs[0].shape[0]
    for e, o in enumerate(outs):
        o_ref[:, e * dv:(e + 1) * dv] = o.T.astype(o_ref.dtype)


def _fin_diff(outs, extra, o_ref, *, lam_init):
    lp = extra[0][...]
    sub_gain = extra[1][...]
    lam = (jnp.exp(jnp.sum(lp[0:1] * lp[1:2], axis=-1, keepdims=True))
           - jnp.exp(jnp.sum(lp[2:3] * lp[3:4], axis=-1, keepdims=True)) + lam_init)
    dv = outs[0].shape[0]
    for e in range(len(outs) // 2):
        y = (outs[2 * e] - lam * outs[2 * e + 1]).T
        y = y * lax.rsqrt(jnp.mean(y * y, axis=-1, keepdims=True) + EPS) * sub_gain
        o_ref[:, e * dv:(e + 1) * dv] = (y * (1.0 - lam_init)).astype(o_ref.dtype)


def flash_attention(q_arr, k_arr, v_arr, *, batch, seq, heads, hp, n_maps, dk, dv,
                    q_blk_off, k_blk_off, v_blk_off, frame_causal,
                    decay_rows=None, extra=(), extra_specs=(), finalize=_fin_heads, name):
    tq = _tile(seq, ATTN_TQ)
    tk = _tile(tq, ATTN_TK)
    assert tk % CHUNK == 0 and tk % V7X_LANES == 0 and heads % hp == 0
    nq = seq // tq
    qw = hp * n_maps * dk
    vw = hp * dv
    decay = decay_rows is not None
    streams = []
    for e in range(hp):
        for g in range(n_maps):
            c = (e * n_maps + g) * dk
            streams.append(_Stream(slice(c, c + dk), slice(c, c + dk), e))
    in_specs = [
        pl.BlockSpec((tq, qw), lambda b, h, i: (b * nq + i, h + q_blk_off)),
        pl.BlockSpec((seq, qw), lambda b, h, i: (b, h + k_blk_off)),
        pl.BlockSpec((seq, vw), lambda b, h, i: (b, h + v_blk_off)),
    ]
    args = [q_arr, k_arr, v_arr]
    scratch = [pltpu.VMEM((hp, dv + SUM_ROWS, seq), BF16)]
    if decay:
        in_specs += [pl.BlockSpec((None, hp, 1, tq), lambda b, h, i: (b, h, 0, i)),
                     pl.BlockSpec((None, hp, 1, seq), lambda b, h, i: (b, h, 0, 0))]
        args += [decay_rows, decay_rows]
        scratch.append(pltpu.VMEM((hp, seq, V7X_LANES), F32))
    in_specs += list(extra_specs)
    args += list(extra)
    body = functools.partial(_flash_body, streams=streams, hp=hp, dv=dv, tq=tq, tk=tk, seq=seq,
                             frame_causal=frame_causal, decay=decay,
                             n_extra=len(extra), finalize=finalize)
    return pl.pallas_call(
        body,
        out_shape=jax.ShapeDtypeStruct((batch * seq, heads * dv), BF16),
        grid=(batch, heads // hp, nq),
        in_specs=in_specs,
        out_specs=pl.BlockSpec((tq, vw), lambda b, h, i: (b * nq + i, h)),
        scratch_shapes=scratch,
        compiler_params=_params(3),
        name=name,
    )(*args)


def _mlstm_body(q_ref, k_ref, v_ref, o_ref, ipc_ref, cumc_ref, gr_ref, gain_ref, y_ref,
                c_ref, n_ref, m_ref, cprev_ref, *, lc, heads, dk):
    h = pl.program_id(1)

    @pl.when(pl.program_id(2) == 0)
    def _():
        c_ref[...] = jnp.zeros_like(c_ref)
        n_ref[...] = jnp.zeros_like(n_ref)
        m_ref[...] = jnp.zeros_like(m_ref)
        cprev_ref[...] = jnp.zeros_like(cprev_ref)

    qscale = dk ** -0.5
    q = q_ref[...]
    k = k_ref[...]
    v = v_ref[...]
    ip_col = ipc_ref[...]
    ip_row = gr_ref[pl.ds(h, 1), :]
    cum_row = gr_ref[pl.ds(heads + h, 1), :]
    cprev = cprev_ref[...]
    m_prev = m_ref[...]
    b_col = cumc_ref[...] - cprev
    b_row = cum_row - cprev
    b_last = b_row[:, lc - 1:lc]

    row = lax.broadcasted_iota(jnp.int32, (lc, lc), 0)
    col = lax.broadcasted_iota(jnp.int32, (lc, lc), 1)
    d = jnp.where(col <= row, b_col - b_row + ip_row, -jnp.inf)
    inter = b_col + m_prev
    m_row = jnp.maximum(jnp.max(d, axis=-1, keepdims=True), inter)
    w_intra = jnp.exp(d - m_row)
    w_inter = jnp.exp(inter - m_row)

    qk = lax.dot_general(q, k, (((1,), (1,)), ((), ())), preferred_element_type=F32) * qscale
    a = w_intra * qk
    c_old = c_ref[...]
    num = (jnp.dot(a.astype(BF16), v, preferred_element_type=F32)
           + w_inter * (jnp.dot(q, c_old.astype(BF16), preferred_element_type=F32) * qscale))
    qn = jnp.sum(q.astype(F32) * n_ref[...], axis=-1, keepdims=True) * qscale
    den = jnp.sum(a, axis=-1, keepdims=True) + w_inter * qn
    hc = num / jnp.maximum(jnp.abs(den), jnp.exp(-m_row))

    g_row = b_last - b_row + ip_row
    g_col = b_last - b_col + ip_col
    m_new = jnp.maximum(b_last + m_prev, jnp.max(g_row, axis=-1, keepdims=True))
    decay = jnp.exp(b_last + m_prev - m_new)
    kw = k.astype(F32) * jnp.exp(g_col - m_new)
    c_ref[...] = decay * c_old + jnp.dot(kw.T.astype(BF16), v, preferred_element_type=F32)
    n_ref[...] = decay * n_ref[...] + jnp.sum(kw, axis=0, keepdims=True)
    m_ref[...] = m_new
    cprev_ref[...] = cum_row[:, lc - 1:lc]

    y = hc * lax.rsqrt(jnp.mean(hc * hc, axis=-1, keepdims=True) + EPS) * gain_ref[...]
    y_ref[...] = (y * jax.nn.sigmoid(o_ref[...])).astype(y_ref.dtype)


def mlstm_chunks(qkv, o, gate_cols, gate_rows, out_gain, *, batch, seq, heads, dk, dv, name):
    lc = _tile(seq, MLSTM_LC)
    nc = seq // lc
    kb, vb = heads, (2 * heads * dk) // dv
    col_spec = lambda off: pl.BlockSpec((None, None, lc, 1), lambda b, h, c: (b, h + off, c, 0))
    in_specs = [
        pl.BlockSpec((lc, dk), lambda b, h, c: (b * nc + c, h)),
        pl.BlockSpec((lc, dk), lambda b, h, c: (b * nc + c, kb + h)),
        pl.BlockSpec((lc, dv), lambda b, h, c: (b * nc + c, vb + h)),
        pl.BlockSpec((lc, dv), lambda b, h, c: (b * nc + c, h)),
        col_spec(0),
        col_spec(heads),
        pl.BlockSpec((None, 2 * heads, lc), lambda b, h, c: (b, 0, c)),
        pl.BlockSpec((None, 1, dv), lambda b, h, c: (h, 0, 0)),
    ]
    return pl.pallas_call(
        functools.partial(_mlstm_body, lc=lc, heads=heads, dk=dk),
        out_shape=jax.ShapeDtypeStruct((batch * seq, heads * dv), BF16),
        grid=(batch, heads, nc),
        in_specs=in_specs,
        out_specs=pl.BlockSpec((lc, dv), lambda b, h, c: (b * nc + c, h)),
        scratch_shapes=[pltpu.VMEM((dk, dv), F32), pltpu.VMEM((1, dk), F32),
                        pltpu.VMEM((1, 1), F32), pltpu.VMEM((1, 1), F32)],
        compiler_params=_params(3),
        name=name,
    )(qkv, qkv, qkv, o, gate_cols, gate_cols, gate_rows, out_gain.reshape(heads, 1, dv))


def _rope_tables(seq, dim):
    pos = jnp.arange(seq, dtype=F32)
    inv = ROPE_THETA ** (-jnp.arange(0, dim, 2, dtype=F32) / dim)
    ang = pos[:, None] * inv[None, :]
    return jnp.cos(ang), jnp.sin(ang)


def _ffn(x, h, w_in_all, w_out_all, lead, name):
    act = mm_swiglu(h, w_in_all, name + "_in", w_lead=lead)
    return mm_residual(act, w_out_all, x, 0.5, name + "_out", w_lead=lead)


def _gate_arrays(gates, n_rows, batch, seq):
    rows = gates.reshape(batch, seq, V7X_LANES)[:, :, :n_rows].transpose(0, 2, 1)
    return rows[..., None], rows


def _fox(x, h, w_in, q_gain, k_gain, f_bias, w_out, batch, seq, name):
    d = x.shape[1]
    hd = d // FOX_HEADS
    hp = 4
    q_scale = hd ** -0.5 * LOG2E
    gain_row = jnp.concatenate([jnp.tile(q_gain * q_scale, FOX_HEADS),
                                jnp.tile(k_gain, FOX_HEADS)])[None, :]
    wb = w_in.astype(BF16)
    qk = mm_headnorm(h, wb, gain_row.astype(F32), name + "_qk")
    v = mm_plain(h, wb, BF16, name + "_v", n_cols=d, col_off=2 * d)
    f = mm_plain(h, _pad_lanes(w_in[:, 3 * d:]).astype(BF16), F32, name + "_f", tn=V7X_LANES)
    mult = jnp.where(jnp.arange(V7X_LANES) < FOX_HEADS, LOG2E, 0.0).astype(F32)[None, :]
    cum = gate_scan(f, _pad_lanes(f_bias[None, :]), mult, batch, seq, name + "_scan")
    _, cum_rows = _gate_arrays(cum, FOX_HEADS, batch, seq)
    o = flash_attention(qk, qk, v, batch=batch, seq=seq, heads=FOX_HEADS, hp=hp, n_maps=1,
                        dk=hd, dv=hd, q_blk_off=0, k_blk_off=FOX_HEADS // hp, v_blk_off=0,
                        frame_causal=True, decay_rows=cum_rows[:, :, None, :],
                        name=name + "_attn")
    return mm_residual(o, w_out.astype(BF16), x, 1.0, name + "_out")


def _diff(x, h, w_in, q_gain, k_gain, lam_p, sub_gain, w_out, lam_init, batch, seq, name):
    d = x.shape[1]
    hd = d // (2 * DIFF_HEADS)
    dv = 2 * hd
    n_maps = 2 * DIFF_HEADS
    cos, sin = _rope_tables(seq, hd)
    tabs = (jnp.concatenate([cos, cos], -1), jnp.concatenate([-sin, sin], -1))
    q_scale = hd ** -0.5 * LOG2E
    gain_row = jnp.concatenate([jnp.tile(q_gain * q_scale, n_maps), jnp.tile(k_gain, n_maps)])[None, :]
    wb = w_in.astype(BF16)
    qk = mm_headnorm(h, wb, gain_row.astype(F32), name + "_qk", rope_tabs=tabs, seq=seq)
    v = mm_plain(h, wb, BF16, name + "_v", n_cols=DIFF_HEADS * dv, col_off=2 * d)
    const = lambda shape: pl.BlockSpec(shape, lambda b, hh, i: (0,) * len(shape))
    hp = 2
    o = flash_attention(qk, qk, v, batch=batch, seq=seq, heads=DIFF_HEADS, hp=hp, n_maps=2,
                        dk=hd, dv=dv, q_blk_off=0, k_blk_off=DIFF_HEADS // hp, v_blk_off=0,
                        frame_causal=False, extra=[lam_p.astype(F32), sub_gain[None, :].astype(F32)],
                        extra_specs=[const((4, hd)), const((1, dv))],
                        finalize=functools.partial(_fin_diff, lam_init=lam_init),
                        name=name + "_attn")
    return mm_residual(o, w_out.astype(BF16), x, 1.0, name + "_out")


def _mla_rope_layout(t):
    half = MLA_ROPE_DIM // 2
    z = jnp.zeros(t.shape[:-1] + (V7X_LANES // 2 - half,), t.dtype)
    return jnp.concatenate([t[..., :half], z, t[..., half:], z], axis=-1)


def _mla(x, h, w_in, q_lat_gain, w_q_up, kv_lat_gain, w_kv_up, q_gain, k_gain, w_out,
         batch, seq, name):
    nh, dn, dr, dv = MLA_HEADS, MLA_NOPE_DIM, MLA_ROPE_DIM, MLA_V_DIM
    q_rank, kv_rank = w_q_up.shape[0], w_kv_up.shape[0]
    cos, sin = _rope_tables(seq, dr)
    z = jnp.zeros_like(cos)
    tabs = (jnp.concatenate([cos, z, cos, z], -1), jnp.concatenate([-sin, z, sin, z], -1))

    wb = w_in.astype(BF16)
    cq = mm_rownorm(h, wb, q_lat_gain[None, :].astype(F32), name + "_cq")
    ckv = mm_rownorm(h, wb, kv_lat_gain[None, :].astype(F32), name + "_ckv", col_off=q_rank)
    kpe = mm_plain(h, _mla_rope_layout(w_in[:, q_rank + kv_rank:]).astype(BF16), F32,
                   name + "_kpe", tn=V7X_LANES)

    def pad_head(t):
        return jnp.concatenate([t[..., :dn], _mla_rope_layout(t[..., dn:])], axis=-1)

    wq = pad_head(w_q_up.reshape(q_rank, nh, dn + dr)).reshape(q_rank, nh * 2 * V7X_LANES)
    t = x.shape[0]
    tm = _tile(t, MM_TM)
    rs = _rope_specs(t, seq, tm)
    qshape, qspec = _out_tile(t, wq.shape[1], BF16, tm, MM_TN)
    row256 = pl.BlockSpec((1, 2 * V7X_LANES), lambda i, j: (0, 0))
    row128 = pl.BlockSpec((1, V7X_LANES), lambda i, j: (0, 0))
    q = matmul(cq, [wq.astype(BF16)], [0], wq.shape[1], tm=tm, tn=MM_TN,
               epilogue=functools.partial(_ep_mla_q, n_valid=dn + dr),
               out_shapes=qshape, out_specs=qspec,
               aux=[pad_head(q_gain * ((dn + dr) ** -0.5 * LOG2E))[None, :].astype(F32),
                    tabs[0], tabs[1]],
               aux_specs=[row256, rs, rs], row_chunk=MM_ROW_CHUNK, name=name + "_q")

    n_kv = w_kv_up.shape[1]
    tn = _tile(n_kv, MM_TN)
    kshape, kspec = _out_tile(t, n_kv, BF16, tm, tn)
    vshape = jax.ShapeDtypeStruct((t, n_kv // 2), BF16)
    vspec = pl.BlockSpec((tm, tn // 2), lambda i, j: (i, j))
    k, v = matmul(ckv, [w_kv_up.astype(BF16)], [0], n_kv, tm=tm, tn=tn,
                  epilogue=functools.partial(_ep_mla_kv, n_valid=dn + dr),
                  out_shapes=(kshape, vshape), out_specs=(kspec, vspec),
                  aux=[kpe, k_gain[None, :dn].astype(F32),
                       _mla_rope_layout(k_gain[dn:])[None, :].astype(F32), tabs[0], tabs[1]],
                  aux_specs=[pl.BlockSpec((tm, V7X_LANES), lambda i, j: (i, 0)), row128, row128, rs, rs],
                  row_chunk=2 * MM_ROW_CHUNK, name=name + "_kv")
    o = flash_attention(q, k, v, batch=batch, seq=seq, heads=nh, hp=4, n_maps=1,
                        dk=2 * V7X_LANES, dv=dv, q_blk_off=0, k_blk_off=0, v_blk_off=0,
                        frame_causal=False, name=name + "_attn")
    return mm_residual(o, w_out.astype(BF16), x, 1.0, name + "_out")


def _mlstm(x, h, w_in, gate_bias, out_gain, w_out, batch, seq, name):
    d = x.shape[1]
    nh = MLSTM_HEADS
    dk, dv = d // (2 * nh), d // nh
    n_qkv = 2 * nh * dk + nh * dv
    wb = w_in.astype(BF16)
    qkv = mm_plain(h, wb, BF16, name + "_qkv", n_cols=n_qkv)
    o = mm_plain(h, wb, F32, name + "_o", n_cols=d, col_off=n_qkv)
    gates = mm_plain(h, _pad_lanes(w_in[:, n_qkv + d:]).astype(BF16), F32, name + "_g", tn=V7X_LANES)
    bias = _pad_lanes(jnp.concatenate([gate_bias[0], gate_bias[1]])[None, :])
    lane = jnp.arange(V7X_LANES)
    mult = ((lane >= nh) & (lane < 2 * nh)).astype(F32)[None, :]
    scanned = gate_scan(gates, bias, mult, batch, seq, name + "_scan")
    gate_cols, gate_rows = _gate_arrays(scanned, 2 * nh, batch, seq)
    y = mlstm_chunks(qkv, o, gate_cols, gate_rows, out_gain.astype(F32), batch=batch, seq=seq,
                     heads=nh, dk=dk, dv=dv, name=name + "_cell")
    return mm_residual(y, w_out.astype(BF16), x, 1.0, name + "_out")


def kernel(x, norm_ffn, ffn_w_in, ffn_w_out, norm_mix, fox_w_in, fox_q_gain, fox_k_gain, fox_f_bias, fox_w_out, diff_w_in, diff_q_gain, diff_k_gain, diff_lambda, diff_sub_gain, diff_w_out, mla_w_in, mla_q_lat_gain, mla_w_q_up, mla_kv_lat_gain, mla_w_kv_up, mla_q_gain, mla_k_gain, mla_w_out, mlstm_w_in, mlstm_gate_bias, mlstm_out_gain, mlstm_w_out):
    batch, seq, d = x.shape
    depth = norm_mix.shape[0]
    x = x.reshape(batch * seq, d)
    ffn_in = (norm_ffn[..., None] * ffn_w_in).astype(BF16)
    ffn_out = ffn_w_out.astype(BF16)
    h = row_stats(x, "rows0")
    for i in range(depth):
        kind, occ = i % N_MIXERS, i // N_MIXERS
        gain = norm_mix[i][:, None]
        x, h = _ffn(x, h, ffn_in, ffn_out, (i, 0), f"l{i}_ffa")
        if kind == 0:
            x, h = _fox(x, h, gain * fox_w_in[occ], fox_q_gain[occ], fox_k_gain[occ],
                        fox_f_bias[occ], fox_w_out[occ], batch, seq, f"l{i}_fox")
        elif kind == 1:
            lam_init = 0.8 - 0.6 * math.exp(-0.3 * i)
            x, h = _diff(x, h, gain * diff_w_in[occ], diff_q_gain[occ], diff_k_gain[occ],
                         diff_lambda[occ], diff_sub_gain[occ], diff_w_out[occ], lam_init,
                         batch, seq, f"l{i}_diff")
        elif kind == 2:
            x, h = _mla(x, h, gain * mla_w_in[occ], mla_q_lat_gain[occ], mla_w_q_up[occ],
                        mla_kv_lat_gain[occ], mla_w_kv_up[occ], mla_q_gain[occ], mla_k_gain[occ],
                        mla_w_out[occ], batch, seq, f"l{i}_mla")
        else:
            x, h = _mlstm(x, h, gain * mlstm_w_in[occ], mlstm_gate_bias[occ], mlstm_out_gain[occ],
                          mlstm_w_out[occ], batch, seq, f"l{i}_mlstm")
        x, h = _ffn(x, h, ffn_in, ffn_out, (i, 1), f"l{i}_ffb")
    return x.reshape(batch, seq, d)
```

```python
import functools
import math
from typing import NamedTuple

import jax
import jax.numpy as jnp
from jax import lax
from jax.experimental import pallas as pl
from jax.experimental.pallas import tpu as pltpu

F32 = jnp.float32
BF16 = jnp.bfloat16

CHUNK = 64
EPS = 1e-6
ROPE_THETA = 10000.0
LOG2E = math.log2(math.e)
N_MIXERS = 4
FOX_HEADS = 32
DIFF_HEADS = 16
MLA_HEADS = 32
MLA_NOPE_DIM = 128
MLA_ROPE_DIM = 64
MLA_V_DIM = 128
MLSTM_HEADS = 8

V7X_LANES = 128
V7X_VMEM_LIMIT_BYTES = 56 * 1024 * 1024

MM_TM = 1024
MM_TN = 1024
MM_ROW_CHUNK = 256
MM_COL_CHUNK = 256
ATTN_TQ = 512
ATTN_TK = 512
SUM_ROWS = 16
MLSTM_LC = 256
NORM_TR = 256
SCAN_BLK = 512


def _tile(n, pref):
    t = min(n, pref)
    assert n % t == 0, (n, pref)
    return t


def _params(n_grid):
    return pltpu.CompilerParams(
        dimension_semantics=("arbitrary",) * n_grid,
        vmem_limit_bytes=V7X_VMEM_LIMIT_BYTES,
    )


class Rows(NamedTuple):
    xb: jax.Array
    ssq: jax.Array


def _row_stats_body(x_ref, xb_ref, ssq_ref):
    x = x_ref[...]
    xb_ref[...] = x.astype(xb_ref.dtype)
    sq = x * x
    part = sq[:, :V7X_LANES]
    for c in range(1, sq.shape[1] // V7X_LANES):
        part = part + sq[:, c * V7X_LANES:(c + 1) * V7X_LANES]
    ssq_ref[...] = part


def row_stats(x, name):
    t, d = x.shape
    tr = _tile(t, NORM_TR)
    xb, ssq = pl.pallas_call(
        _row_stats_body,
        out_shape=(jax.ShapeDtypeStruct((t, d), BF16), jax.ShapeDtypeStruct((t, V7X_LANES), F32)),
        grid=(t // tr,),
        in_specs=[pl.BlockSpec((tr, d), lambda i: (i, 0))],
        out_specs=(pl.BlockSpec((tr, d), lambda i: (i, 0)),
                   pl.BlockSpec((tr, V7X_LANES), lambda i: (i, 0))),
        compiler_params=_params(1),
        name=name,
    )(x)
    return Rows(xb, ssq)


def _mm_body(*refs, normed, n_w, n_aux, epilogue, row_chunk, col_chunk, n_acc_outs):
    a_ref = refs[0]
    refs = refs[1:]
    if normed:
        ssq_ref, refs = refs[0], refs[1:]
    w_refs = refs[:n_w]
    aux_refs = refs[n_w:n_w + n_aux]
    out_refs = refs[n_w + n_aux:]
    tm, tn = a_ref.shape[0], w_refs[0].shape[1]

    @pl.when(pl.program_id(1) == 0)
    def _():
        for o in out_refs[len(out_refs) - n_acc_outs:]:
            o[...] = jnp.zeros_like(o)

    for r0 in range(0, tm, row_chunk):
        rows = slice(r0, r0 + row_chunk)
        a = a_ref[rows, :]
        if normed:
            ssq = jnp.sum(ssq_ref[rows, :], axis=-1, keepdims=True)
            r = lax.rsqrt(ssq * (1.0 / a.shape[1]) + EPS)
        for c0 in range(0, tn, col_chunk):
            cols = slice(c0, c0 + col_chunk)

            def view(ref):
                ref = ref.at[rows] if ref.shape[0] == tm else ref
                return ref.at[:, cols] if ref.shape[1] == tn else ref

            accs = [jnp.dot(a, w[:, cols], preferred_element_type=F32) for w in w_refs]
            if normed:
                accs = [acc * r for acc in accs]
            epilogue(accs, [view(x) for x in aux_refs], [view(x) for x in out_refs])


def matmul(a, ws, w_blk_offsets, n_cols, *, tm, tn, epilogue, out_shapes, out_specs,
           aux=(), aux_specs=(), w_lead=(), row_chunk=None, col_chunk=None, n_acc_outs=0, name):
    normed = isinstance(a, Rows)
    a_arr = a.xb if normed else a
    t, k = a_arr.shape
    tm = _tile(t, tm)
    tn = _tile(n_cols, tn)
    in_specs = [pl.BlockSpec((tm, k), lambda i, j: (i, 0))]
    args = [a_arr]
    if normed:
        in_specs.append(pl.BlockSpec((tm, V7X_LANES), lambda i, j: (i, 0)))
        args.append(a.ssq)
    lead_blk = (None,) * len(w_lead)
    for off in w_blk_offsets:
        in_specs.append(pl.BlockSpec(lead_blk + (k, tn),
                                     lambda i, j, off=off: tuple(w_lead) + (0, j + off)))
    in_specs += list(aux_specs)
    body = functools.partial(_mm_body, normed=normed, n_w=len(ws), n_aux=len(aux),
                             epilogue=epilogue, row_chunk=_tile(tm, row_chunk or tm),
                             col_chunk=_tile(tn, col_chunk or tn), n_acc_outs=n_acc_outs)
    return pl.pallas_call(
        body,
        out_shape=out_shapes,
        grid=(t // tm, n_cols // tn),
        in_specs=in_specs,
        out_specs=out_specs,
        compiler_params=_params(2),
        name=name,
    )(*args, *ws, *aux)


def _ep_plain(accs, aux, outs):
    outs[0][...] = accs[0].astype(outs[0].dtype)


def _ep_swiglu(accs, aux, outs):
    g, u = accs
    outs[0][...] = (g * jax.nn.sigmoid(g) * u).astype(outs[0].dtype)


def _ep_residual(accs, aux, outs, *, alpha):
    x_out, xb_out, ssq_out = outs
    x_new = aux[0][...] + alpha * accs[0]
    x_out[...] = x_new
    xb_out[...] = x_new.astype(xb_out.dtype)
    sq = x_new * x_new
    part = sq[:, :V7X_LANES]
    for c in range(1, sq.shape[1] // V7X_LANES):
        part = part + sq[:, c * V7X_LANES:(c + 1) * V7X_LANES]
    ssq_out[...] += part


def _rope(t, cosf, sinf):
    return t * cosf + pltpu.roll(t, V7X_LANES // 2, 1) * sinf


def _ep_headnorm(accs, aux, outs, *, rope):
    acc = accs[0]
    if rope:
        cosf, sinf = aux[1][...], aux[2][...]
    for c in range(acc.shape[1] // V7X_LANES):
        sl = slice(c * V7X_LANES, (c + 1) * V7X_LANES)
        t = acc[:, sl]
        y = t * lax.rsqrt(jnp.mean(t * t, axis=-1, keepdims=True) + EPS) * aux[0][:, sl]
        if rope:
            y = _rope(y, cosf, sinf)
        outs[0][:, sl] = y.astype(outs[0].dtype)


def _ep_rownorm(accs, aux, outs):
    t = accs[0]
    y = t * lax.rsqrt(jnp.mean(t * t, axis=-1, keepdims=True) + EPS) * aux[0][...]
    outs[0][...] = y.astype(outs[0].dtype)


def _ep_mla_q(accs, aux, outs, *, n_valid):
    acc = accs[0]
    gain, cosf, sinf = aux[0][...], aux[1][...], aux[2][...]
    w = 2 * V7X_LANES
    for c in range(acc.shape[1] // w):
        t = acc[:, c * w:(c + 1) * w]
        r = lax.rsqrt(jnp.sum(t * t, axis=-1, keepdims=True) / n_valid + EPS)
        y = t * r * gain
        o = c * w
        outs[0][:, o:o + V7X_LANES] = y[:, :V7X_LANES].astype(outs[0].dtype)
        outs[0][:, o + V7X_LANES:o + w] = _rope(y[:, V7X_LANES:], cosf, sinf).astype(outs[0].dtype)


def _ep_mla_kv(accs, aux, outs, *, n_valid):
    acc = accs[0]
    kpe, gain_nope, gain_pe, cosf, sinf = (r[...] for r in aux)
    k_out, v_out = outs
    ssq_pe = jnp.sum(kpe * kpe, axis=-1, keepdims=True)
    pe_rot = _rope(kpe * gain_pe, cosf, sinf)
    w = 2 * V7X_LANES
    for c in range(acc.shape[1] // w):
        kn = acc[:, c * w:c * w + V7X_LANES]
        v = acc[:, c * w + V7X_LANES:(c + 1) * w]
        r = lax.rsqrt((jnp.sum(kn * kn, axis=-1, keepdims=True) + ssq_pe) / n_valid + EPS)
        o = c * w
        k_out[:, o:o + V7X_LANES] = (kn * r * gain_nope).astype(k_out.dtype)
        k_out[:, o + V7X_LANES:o + w] = (pe_rot * r).astype(k_out.dtype)
        v_out[:, o // 2:o // 2 + V7X_LANES] = v.astype(v_out.dtype)


def _out_tile(t, n, dtype, tm, tn):
    tm, tn = _tile(t, tm), _tile(n, tn)
    return jax.ShapeDtypeStruct((t, n), dtype), pl.BlockSpec((tm, tn), lambda i, j: (i, j))


def _col_blk(col_off, n, tn):
    tn = _tile(n, tn)
    assert col_off % tn == 0, (col_off, tn)
    return col_off // tn


def _n_rows(a):
    return (a.xb if isinstance(a, Rows) else a).shape[0]


def mm_plain(a, w, out_dtype, name, n_cols=None, col_off=0, tm=MM_TM, tn=MM_TN):
    t, n = _n_rows(a), n_cols or w.shape[1]
    shape, spec = _out_tile(t, n, out_dtype, tm, tn)
    return matmul(a, [w], [_col_blk(col_off, n, tn)], n, tm=tm, tn=tn, epilogue=_ep_plain,
                  out_shapes=shape, out_specs=spec, name=name)


def mm_swiglu(a, w_in, name, w_lead=(), tm=MM_TM, tn=MM_TN // 2):
    t, n = _n_rows(a), w_in.shape[-1] // 2
    tn = _tile(n, tn)
    shape, spec = _out_tile(t, n, BF16, tm, tn)
    return matmul(a, [w_in, w_in], [0, n // tn], n, tm=tm, tn=tn, epilogue=_ep_swiglu,
                  out_shapes=shape, out_specs=spec, w_lead=w_lead, name=name)


def mm_residual(a, w, res, alpha, name, w_lead=(), tm=MM_TM, tn=MM_TN // 2):
    t, n = _n_rows(a), w.shape[-1]
    tm_ = _tile(t, tm)
    shape, spec = _out_tile(t, n, F32, tm, tn)
    x, xb, ssq = matmul(
        a, [w], [0], n, tm=tm, tn=tn, epilogue=functools.partial(_ep_residual, alpha=alpha),
        out_shapes=(shape, jax.ShapeDtypeStruct((t, n), BF16),
                    jax.ShapeDtypeStruct((t, V7X_LANES), F32)),
        out_specs=(spec, spec, pl.BlockSpec((tm_, V7X_LANES), lambda i, j: (i, 0))),
        aux=[res], aux_specs=[spec], w_lead=w_lead, col_chunk=MM_COL_CHUNK, n_acc_outs=1, name=name)
    return x, Rows(xb, ssq)


def _rope_specs(t, seq, tm):
    tm = _tile(t, tm)
    assert seq % tm == 0
    nblk = seq // tm
    return pl.BlockSpec((tm, V7X_LANES), lambda i, j: (i % nblk, 0))


def mm_headnorm(a, w, gain_row, name, rope_tabs=None, seq=None, tm=MM_TM, tn=MM_TN):
    t, n = _n_rows(a), gain_row.shape[1]
    tn_ = _tile(n, tn)
    shape, spec = _out_tile(t, n, BF16, tm, tn)
    aux = [gain_row]
    aux_specs = [pl.BlockSpec((1, tn_), lambda i, j: (0, j))]
    if rope_tabs is not None:
        rs = _rope_specs(t, seq, tm)
        aux += list(rope_tabs)
        aux_specs += [rs, rs]
    return matmul(a, [w], [0], n, tm=tm, tn=tn,
                  epilogue=functools.partial(_ep_headnorm, rope=rope_tabs is not None),
                  out_shapes=shape, out_specs=spec, aux=aux, aux_specs=aux_specs,
                  row_chunk=MM_ROW_CHUNK, name=name)


def mm_rownorm(a, w, gain_row, name, col_off=0, tm=MM_TM):
    t, n = _n_rows(a), gain_row.shape[1]
    shape, spec = _out_tile(t, n, BF16, tm, n)
    return matmul(a, [w], [_col_blk(col_off, n, n)], n, tm=tm, tn=n, epilogue=_ep_rownorm,
                  out_shapes=shape, out_specs=spec, aux=[gain_row],
                  aux_specs=[pl.BlockSpec((1, n), lambda i, j: (0, 0))], name=name)


def _gate_scan_body(x_ref, bias_ref, mult_ref, o_ref, carry_ref, *, blk):
    @pl.when(pl.program_id(1) == 0)
    def _():
        carry_ref[...] = jnp.zeros_like(carry_ref)

    y = x_ref[...] + bias_ref[...]
    c = jnp.minimum(y, 0.0) - jnp.log1p(jnp.exp(-jnp.abs(y)))
    row = lax.broadcasted_iota(jnp.int32, c.shape, 0)
    shift = 1
    while shift < blk:
        c = c + jnp.where(row >= shift, pltpu.roll(c, shift, 0), 0.0)
        shift *= 2
    c = c + carry_ref[...]
    carry_ref[...] = c[blk - 1:blk, :]
    mult = mult_ref[...]
    o_ref[...] = jnp.where(mult != 0.0, c * mult, y)


def gate_scan(x, bias_row, mult_row, batch, seq, name):
    blk = _tile(seq, SCAN_BLK)
    nblk = seq // blk
    spec = pl.BlockSpec((blk, V7X_LANES), lambda b, s: (b * nblk + s, 0))
    row = pl.BlockSpec((1, V7X_LANES), lambda b, s: (0, 0))
    return pl.pallas_call(
        functools.partial(_gate_scan_body, blk=blk),
        out_shape=jax.ShapeDtypeStruct(x.shape, F32),
        grid=(batch, nblk),
        in_specs=[spec, row, row],
        out_specs=spec,
        scratch_shapes=[pltpu.VMEM((1, V7X_LANES), F32)],
        compiler_params=_params(2),
        name=name,
    )(x, bias_row, mult_row)


def _pad_lanes(v, width=V7X_LANES):
    return jnp.pad(v, [(0, 0)] * (v.ndim - 1) + [(0, width - v.shape[-1])])


class _Stream:
    def __init__(self, q_cols, k_cols, head):
        self.q_cols, self.k_cols, self.head = q_cols, k_cols, head


def _flash_body(*refs, streams, hp, dv, tq, tk, seq, frame_causal, decay, n_extra, finalize):
    q_ref, k_ref, v_ref = refs[:3]
    pos = 3
    if decay:
        cq_ref, ck_ref = refs[3:5]
        pos = 5
    extra = refs[pos:pos + n_extra]
    o_ref = refs[pos + n_extra]
    vt_ref = refs[pos + n_extra + 1]
    ckt_ref = refs[pos + n_extra + 2] if decay else None
    qi = pl.program_id(2)

    @pl.when(qi == 0)
    def _():
        def fill(c, _):
            start = pl.multiple_of(c * tq, tq)
            ones_rows = (lax.broadcasted_iota(jnp.int32, (SUM_ROWS, tq), 0) == 0).astype(vt_ref.dtype)
            for e in range(hp):
                vt_ref[e, :dv, pl.ds(start, tq)] = v_ref[pl.ds(start, tq), e * dv:(e + 1) * dv].T
                vt_ref[e, dv:, pl.ds(start, tq)] = ones_rows
                if decay:
                    ck = jnp.broadcast_to(ck_ref[e, :, pl.ds(start, tq)], (V7X_LANES, tq))
                    ckt_ref[e, pl.ds(start, tq), :] = ck.T
            return 0
        lax.fori_loop(0, seq // tq, fill, 0)

    def diag_mask(d):
        key = lax.broadcasted_iota(jnp.int32, (tk, tq), 0) + d * tk
        qry = lax.broadcasted_iota(jnp.int32, (tk, tq), 1)
        if frame_causal:
            return key <= qry
        shift = CHUNK.bit_length() - 1
        return jnp.right_shift(key, shift) <= jnp.right_shift(qry, shift)

    qs = [q_ref[:, st.q_cols] for st in streams]
    cqs = [cq_ref[st.head] if decay else None for st in streams]

    def block(j, state, mask):
        start = pl.multiple_of(j * tk, tk)
        ts = [lax.dot_general(k_ref[pl.ds(start, tk), st.k_cols], q, (((1,), (1,)), ((), ())),
                              preferred_element_type=F32) for st, q in zip(streams, qs)]
        out = []
        for st, t, cq, (m, acc) in zip(streams, ts, cqs, state):
            vt = vt_ref[st.head, :, pl.ds(start, tk)]
            if decay:
                ckb = ckt_ref[st.head, pl.ds(start, tk), :]
                t = t - jnp.concatenate([ckb] * (tq // V7X_LANES), axis=1)
            if mask is not None:
                t = jnp.where(mask, t, -jnp.inf)
            cmax = jnp.max(t, axis=0, keepdims=True)
            if decay:
                m_new = jnp.maximum(m, cmax + cq)
                mt = m_new - cq
            else:
                m_new = jnp.maximum(m, cmax)
                mt = m_new
            alpha = jnp.exp2(m - m_new)
            p = jnp.exp2(t - mt)
            acc = alpha * acc + jnp.dot(vt, p.astype(BF16), preferred_element_type=F32)
            out.append((m_new, acc))
        return tuple(out)

    init = tuple((jnp.full((1, tq), -jnp.inf, F32), jnp.zeros((dv + SUM_ROWS, tq), F32))
                 for _ in streams)
    n_diag = tq // tk
    n_full = qi * n_diag
    state = lax.fori_loop(0, n_full, lambda j, s: block(j, s, None), init)
    for d in range(n_diag):
        state = block(n_full + d, state, diag_mask(d))
    finalize([acc[:dv] / acc[dv:dv + 1] for _, acc in state], extra, o_ref)


def _fin_heads(outs, extra, o_ref):
    dv = outs[0].shape[0]
    for e, o in enumerate(outs):
        o_ref[:, e * dv:(e + 1) * dv] = o.T.astype(o_ref.dtype)


def _fin_diff(outs, extra, o_ref, *, lam_init):
    lp = extra[0][...]
    sub_gain = extra[1][...]
    lam = (jnp.exp(jnp.sum(lp[0:1] * lp[1:2], axis=-1, keepdims=True))
           - jnp.exp(jnp.sum(lp[2:3] * lp[3:4], axis=-1, keepdims=True)) + lam_init)
    dv = outs[0].shape[0]
    for e in range(len(outs) // 2):
        y = (outs[2 * e] - lam * outs[2 * e + 1]).T
        y = y * lax.rsqrt(jnp.mean(y * y, axis=-1, keepdims=True) + EPS) * sub_gain
        o_ref[:, e * dv:(e + 1) * dv] = (y * (1.0 - lam_init)).astype(o_ref.dtype)


def flash_attention(q_arr, k_arr, v_arr, *, batch, seq, heads, hp, n_maps, dk, dv,
                    q_blk_off, k_blk_off, v_blk_off, frame_causal,
                    decay_rows=None, extra=(), extra_specs=(), finalize=_fin_heads, name):
    tq = _tile(seq, ATTN_TQ)
    tk = _tile(tq, ATTN_TK)
    assert tk % CHUNK == 0 and tk % V7X_LANES == 0 and heads % hp == 0
    nq = seq // tq
    qw = hp * n_maps * dk
    vw = hp * dv
    decay = decay_rows is not None
    streams = []
    for e in range(hp):
        for g in range(n_maps):
            c = (e * n_maps + g) * dk
            streams.append(_Stream(slice(c, c + dk), slice(c, c + dk), e))
    in_specs = [
        pl.BlockSpec((tq, qw), lambda b, h, i: (b * nq + i, h + q_blk_off)),
        pl.BlockSpec((seq, qw), lambda b, h, i: (b, h + k_blk_off)),
        pl.BlockSpec((seq, vw), lambda b, h, i: (b, h + v_blk_off)),
    ]
    args = [q_arr, k_arr, v_arr]
    scratch = [pltpu.VMEM((hp, dv + SUM_ROWS, seq), BF16)]
    if decay:
        in_specs += [pl.BlockSpec((None, hp, 1, tq), lambda b, h, i: (b, h, 0, i)),
                     pl.BlockSpec((None, hp, 1, seq), lambda b, h, i: (b, h, 0, 0))]
        args += [decay_rows, decay_rows]
        scratch.append(pltpu.VMEM((hp, seq, V7X_LANES), F32))
    in_specs += list(extra_specs)
    args += list(extra)
    body = functools.partial(_flash_body, streams=streams, hp=hp, dv=dv, tq=tq, tk=tk, seq=seq,
                             frame_causal=frame_causal, decay=decay,
                             n_extra=len(extra), finalize=finalize)
    return pl.pallas_call(
        body,
        out_shape=jax.ShapeDtypeStruct((batch * seq, heads * dv), BF16),
        grid=(batch, heads // hp, nq),
        in_specs=in_specs,
        out_specs=pl.BlockSpec((tq, vw), lambda b, h, i: (b * nq + i, h)),
        scratch_shapes=scratch,
        compiler_params=_params(3),
        name=name,
    )(*args)


def _mlstm_body(q_ref, k_ref, v_ref, o_ref, ipc_ref, cumc_ref, gr_ref, gain_ref, y_ref,
                c_ref, n_ref, m_ref, cprev_ref, *, lc, heads, dk):
    h = pl.program_id(1)

    @pl.when(pl.program_id(2) == 0)
    def _():
        c_ref[...] = jnp.zeros_like(c_ref)
        n_ref[...] = jnp.zeros_like(n_ref)
        m_ref[...] = jnp.zeros_like(m_ref)
        cprev_ref[...] = jnp.zeros_like(cprev_ref)

    qscale = dk ** -0.5
    q = q_ref[...]
    k = k_ref[...]
    v = v_ref[...]
    ip_col = ipc_ref[...]
    ip_row = gr_ref[pl.ds(h, 1), :]
    cum_row = gr_ref[pl.ds(heads + h, 1), :]
    cprev = cprev_ref[...]
    m_prev = m_ref[...]
    b_col = cumc_ref[...] - cprev
    b_row = cum_row - cprev
    b_last = b_row[:, lc - 1:lc]

    row = lax.broadcasted_iota(jnp.int32, (lc, lc), 0)
    col = lax.broadcasted_iota(jnp.int32, (lc, lc), 1)
    d = jnp.where(col <= row, b_col - b_row + ip_row, -jnp.inf)
    inter = b_col + m_prev
    m_row = jnp.maximum(jnp.max(d, axis=-1, keepdims=True), inter)
    w_intra = jnp.exp(d - m_row)
    w_inter = jnp.exp(inter - m_row)

    qk = lax.dot_general(q, k, (((1,), (1,)), ((), ())), preferred_element_type=F32) * qscale
    a = w_intra * qk
    c_old = c_ref[...]
    num = (jnp.dot(a.astype(BF16), v, preferred_element_type=F32)
           + w_inter * (jnp.dot(q, c_old.astype(BF16), preferred_element_type=F32) * qscale))
    qn = jnp.sum(q.astype(F32) * n_ref[...], axis=-1, keepdims=True) * qscale
    den = jnp.sum(a, axis=-1, keepdims=True) + w_inter * qn
    hc = num / jnp.maximum(jnp.abs(den), jnp.exp(-m_row))

    g_row = b_last - b_row + ip_row
    g_col = b_last - b_col + ip_col
    m_new = jnp.maximum(b_last + m_prev, jnp.max(g_row, axis=-1, keepdims=True))
    decay = jnp.exp(b_last + m_prev - m_new)
    kw = k.astype(F32) * jnp.exp(g_col - m_new)
    c_ref[...] = decay * c_old + jnp.dot(kw.T.astype(BF16), v, preferred_element_type=F32)
    n_ref[...] = decay * n_ref[...] + jnp.sum(kw, axis=0, keepdims=True)
    m_ref[...] = m_new
    cprev_ref[...] = cum_row[:, lc - 1:lc]

    y = hc * lax.rsqrt(jnp.mean(hc * hc, axis=-1, keepdims=True) + EPS) * gain_ref[...]
    y_ref[...] = (y * jax.nn.sigmoid(o_ref[...])).astype(y_ref.dtype)


def mlstm_chunks(qkv, o, gate_cols, gate_rows, out_gain, *, batch, seq, heads, dk, dv, name):
    lc = _tile(seq, MLSTM_LC)
    nc = seq // lc
    kb, vb = heads, (2 * heads * dk) // dv
    col_spec = lambda off: pl.BlockSpec((None, None, lc, 1), lambda b, h, c: (b, h + off, c, 0))
    in_specs = [
        pl.BlockSpec((lc, dk), lambda b, h, c: (b * nc + c, h)),
        pl.BlockSpec((lc, dk), lambda b, h, c: (b * nc + c, kb + h)),
        pl.BlockSpec((lc, dv), lambda b, h, c: (b * nc + c, vb + h)),
        pl.BlockSpec((lc, dv), lambda b, h, c: (b * nc + c, h)),
        col_spec(0),
        col_spec(heads),
        pl.BlockSpec((None, 2 * heads, lc), lambda b, h, c: (b, 0, c)),
        pl.BlockSpec((None, 1, dv), lambda b, h, c: (h, 0, 0)),
    ]
    return pl.pallas_call(
        functools.partial(_mlstm_body, lc=lc, heads=heads, dk=dk),
        out_shape=jax.ShapeDtypeStruct((batch * seq, heads * dv), BF16),
        grid=(batch, heads, nc),
        in_specs=in_specs,
        out_specs=pl.BlockSpec((lc, dv), lambda b, h, c: (b * nc + c, h)),
        scratch_shapes=[pltpu.VMEM((dk, dv), F32), pltpu.VMEM((1, dk), F32),
                        pltpu.VMEM((1, 1), F32), pltpu.VMEM((1, 1), F32)],
        compiler_params=_params(3),
        name=name,
    )(qkv, qkv, qkv, o, gate_cols, gate_cols, gate_rows, out_gain.reshape(heads, 1, dv))


def _rope_tables(seq, dim):
    pos = jnp.arange(seq, dtype=F32)
    inv = ROPE_THETA ** (-jnp.arange(0, dim, 2, dtype=F32) / dim)
    ang = pos[:, None] * inv[None, :]
    return jnp.cos(ang), jnp.sin(ang)


def _ffn(x, h, w_in_all, w_out_all, lead, name):
    act = mm_swiglu(h, w_in_all, name + "_in", w_lead=lead)
    return mm_residual(act, w_out_all, x, 0.5, name + "_out", w_lead=lead)


def _gate_arrays(gates, n_rows, batch, seq):
    rows = gates.reshape(batch, seq, V7X_LANES)[:, :, :n_rows].transpose(0, 2, 1)
    return rows[..., None], rows


def _fox(x, h, w_in, q_gain, k_gain, f_bias, w_out, batch, seq, name):
    d = x.shape[1]
    hd = d // FOX_HEADS
    hp = 4
    q_scale = hd ** -0.5 * LOG2E
    gain_row = jnp.concatenate([jnp.tile(q_gain * q_scale, FOX_HEADS),
                                jnp.tile(k_gain, FOX_HEADS)])[None, :]
    wb = w_in.astype(BF16)
    qk = mm_headnorm(h, wb, gain_row.astype(F32), name + "_qk")
    v = mm_plain(h, wb, BF16, name + "_v", n_cols=d, col_off=2 * d)
    f = mm_plain(h, _pad_lanes(w_in[:, 3 * d:]).astype(BF16), F32, name + "_f", tn=V7X_LANES)
    mult = jnp.where(jnp.arange(V7X_LANES) < FOX_HEADS, LOG2E, 0.0).astype(F32)[None, :]
    cum = gate_scan(f, _pad_lanes(f_bias[None, :]), mult, batch, seq, name + "_scan")
    _, cum_rows = _gate_arrays(cum, FOX_HEADS, batch, seq)
    o = flash_attention(qk, qk, v, batch=batch, seq=seq, heads=FOX_HEADS, hp=hp, n_maps=1,
                        dk=hd, dv=hd, q_blk_off=0, k_blk_off=FOX_HEADS // hp, v_blk_off=0,
                        frame_causal=True, decay_rows=cum_rows[:, :, None, :],
                        name=name + "_attn")
    return mm_residual(o, w_out.astype(BF16), x, 1.0, name + "_out")


def _diff(x, h, w_in, q_gain, k_gain, lam_p, sub_gain, w_out, lam_init, batch, seq, name):
    d = x.shape[1]
    hd = d // (2 * DIFF_HEADS)
    dv = 2 * hd
    n_maps = 2 * DIFF_HEADS
    cos, sin = _rope_tables(seq, hd)
    tabs = (jnp.concatenate([cos, cos], -1), jnp.concatenate([-sin, sin], -1))
    q_scale = hd ** -0.5 * LOG2E
    gain_row = jnp.concatenate([jnp.tile(q_gain * q_scale, n_maps), jnp.tile(k_gain, n_maps)])[None, :]
    wb = w_in.astype(BF16)
    qk = mm_headnorm(h, wb, gain_row.astype(F32), name + "_qk", rope_tabs=tabs, seq=seq)
    v = mm_plain(h, wb, BF16, name + "_v", n_cols=DIFF_HEADS * dv, col_off=2 * d)
    const = lambda shape: pl.BlockSpec(shape, lambda b, hh, i: (0,) * len(shape))
    hp = 2
    o = flash_attention(qk, qk, v, batch=batch, seq=seq, heads=DIFF_HEADS, hp=hp, n_maps=2,
                        dk=hd, dv=dv, q_blk_off=0, k_blk_off=DIFF_HEADS // hp, v_blk_off=0,
                        frame_causal=False, extra=[lam_p.astype(F32), sub_gain[None, :].astype(F32)],
                        extra_specs=[const((4, hd)), const((1, dv))],
                        finalize=functools.partial(_fin_diff, lam_init=lam_init),
                        name=name + "_attn")
    return mm_residual(o, w_out.astype(BF16), x, 1.0, name + "_out")


def _mla_rope_layout(t):
    half = MLA_ROPE_DIM // 2
    z = jnp.zeros(t.shape[:-1] + (V7X_LANES // 2 - half,), t.dtype)
    return jnp.concatenate([t[..., :half], z, t[..., half:], z], axis=-1)


def _mla(x, h, w_in, q_lat_gain, w_q_up, kv_lat_gain, w_kv_up, q_gain, k_gain, w_out,
         batch, seq, name):
    nh, dn, dr, dv = MLA_HEADS, MLA_NOPE_DIM, MLA_ROPE_DIM, MLA_V_DIM
    q_rank, kv_rank = w_q_up.shape[0], w_kv_up.shape[0]
    cos, sin = _rope_tables(seq, dr)
    z = jnp.zeros_like(cos)
    tabs = (jnp.concatenate([cos, z, cos, z], -1), jnp.concatenate([-sin, z, sin, z], -1))

    wb = w_in.astype(BF16)
    cq = mm_rownorm(h, wb, q_lat_gain[None, :].astype(F32), name + "_cq")
    ckv = mm_rownorm(h, wb, kv_lat_gain[None, :].astype(F32), name + "_ckv", col_off=q_rank)
    kpe = mm_plain(h, _mla_rope_layout(w_in[:, q_rank + kv_rank:]).astype(BF16), F32,
                   name + "_kpe", tn=V7X_LANES)

    def pad_head(t):
        return jnp.concatenate([t[..., :dn], _mla_rope_layout(t[..., dn:])], axis=-1)

    wq = pad_head(w_q_up.reshape(q_rank, nh, dn + dr)).reshape(q_rank, nh * 2 * V7X_LANES)
    t = x.shape[0]
    tm = _tile(t, MM_TM)
    rs = _rope_specs(t, seq, tm)
    qshape, qspec = _out_tile(t, wq.shape[1], BF16, tm, MM_TN)
    row256 = pl.BlockSpec((1, 2 * V7X_LANES), lambda i, j: (0, 0))
    row128 = pl.BlockSpec((1, V7X_LANES), lambda i, j: (0, 0))
    q = matmul(cq, [wq.astype(BF16)], [0], wq.shape[1], tm=tm, tn=MM_TN,
               epilogue=functools.partial(_ep_mla_q, n_valid=dn + dr),
               out_shapes=qshape, out_specs=qspec,
               aux=[pad_head(q_gain * ((dn + dr) ** -0.5 * LOG2E))[None, :].astype(F32),
                    tabs[0], tabs[1]],
               aux_specs=[row256, rs, rs], row_chunk=MM_ROW_CHUNK, name=name + "_q")

    n_kv = w_kv_up.shape[1]
    tn = _tile(n_kv, MM_TN)
    kshape, kspec = _out_tile(t, n_kv, BF16, tm, tn)
    vshape = jax.ShapeDtypeStruct((t, n_kv // 2), BF16)
    vspec = pl.BlockSpec((tm, tn // 2), lambda i, j: (i, j))
    k, v = matmul(ckv, [w_kv_up.astype(BF16)], [0], n_kv, tm=tm, tn=tn,
                  epilogue=functools.partial(_ep_mla_kv, n_valid=dn + dr),
                  out_shapes=(kshape, vshape), out_specs=(kspec, vspec),
                  aux=[kpe, k_gain[None, :dn].astype(F32),
                       _mla_rope_layout(k_gain[dn:])[None, :].astype(F32), tabs[0], tabs[1]],
                  aux_specs=[pl.BlockSpec((tm, V7X_LANES), lambda i, j: (i, 0)), row128, row128, rs, rs],
                  row_chunk=2 * MM_ROW_CHUNK, name=name + "_kv")
    o = flash_attention(q, k, v, batch=batch, seq=seq, heads=nh, hp=4, n_maps=1,
                        dk=2 * V7X_LANES, dv=dv, q_blk_off=0, k_blk_off=0, v_blk_off=0,
                        frame_causal=False, name=name + "_attn")
    return mm_residual(o, w_out.astype(BF16), x, 1.0, name + "_out")


def _mlstm(x, h, w_in, gate_bias, out_gain, w_out, batch, seq, name):
    d = x.shape[1]
    nh = MLSTM_HEADS
    dk, dv = d // (2 * nh), d // nh
    n_qkv = 2 * nh * dk + nh * dv
    wb = w_in.astype(BF16)
    qkv = mm_plain(h, wb, BF16, name + "_qkv", n_cols=n_qkv)
    o = mm_plain(h, wb, F32, name + "_o", n_cols=d, col_off=n_qkv)
    gates = mm_plain(h, _pad_lanes(w_in[:, n_qkv + d:]).astype(BF16), F32, name + "_g", tn=V7X_LANES)
    bias = _pad_lanes(jnp.concatenate([gate_bias[0], gate_bias[1]])[None, :])
    lane = jnp.arange(V7X_LANES)
    mult = ((lane >= nh) & (lane < 2 * nh)).astype(F32)[None, :]
    scanned = gate_scan(gates, bias, mult, batch, seq, name + "_scan")
    gate_cols, gate_rows = _gate_arrays(scanned, 2 * nh, batch, seq)
    y = mlstm_chunks(qkv, o, gate_cols, gate_rows, out_gain.astype(F32), batch=batch, seq=seq,
                     heads=nh, dk=dk, dv=dv, name=name + "_cell")
    return mm_residual(y, w_out.astype(BF16), x, 1.0, name + "_out")


def kernel(x, norm_ffn, ffn_w_in, ffn_w_out, norm_mix, fox_w_in, fox_q_gain, fox_k_gain, fox_f_bias, fox_w_out, diff_w_in, diff_q_gain, diff_k_gain, diff_lambda, diff_sub_gain, diff_w_out, mla_w_in, mla_q_lat_gain, mla_w_q_up, mla_kv_lat_gain, mla_w_kv_up, mla_q_gain, mla_k_gain, mla_w_out, mlstm_w_in, mlstm_gate_bias, mlstm_out_gain, mlstm_w_out):
    batch, seq, d = x.shape
    depth = norm_mix.shape[0]
    x = x.reshape(batch * seq, d)
    ffn_in = (norm_ffn[..., None] * ffn_w_in).astype(BF16)
    ffn_out = ffn_w_out.astype(BF16)
    h = row_stats(x, "rows0")
    for i in range(depth):
        kind, occ = i % N_MIXERS, i // N_MIXERS
        gain = norm_mix[i][:, None]
        x, h = _ffn(x, h, ffn_in, ffn_out, (i, 0), f"l{i}_ffa")
        if kind == 0:
            x, h = _fox(x, h, gain * fox_w_in[occ], fox_q_gain[occ], fox_k_gain[occ],
                        fox_f_bias[occ], fox_w_out[occ], batch, seq, f"l{i}_fox")
        elif kind == 1:
            lam_init = 0.8 - 0.6 * math.exp(-0.3 * i)
            x, h = _diff(x, h, gain * diff_w_in[occ], diff_q_gain[occ], diff_k_gain[occ],
                         diff_lambda[occ], diff_sub_gain[occ], diff_w_out[occ], lam_init,
                         batch, seq, f"l{i}_diff")
        elif kind == 2:
            x, h = _mla(x, h, gain * mla_w_in[occ], mla_q_lat_gain[occ], mla_w_q_up[occ],
                        mla_kv_lat_gain[occ], mla_w_kv_up[occ], mla_q_gain[occ], mla_k_gain[occ],
                        mla_w_out[occ], batch, seq, f"l{i}_mla")
        else:
            x, h = _mlstm(x, h, gain * mlstm_w_in[occ], mlstm_gate_bias[occ], mlstm_out_gain[occ],
                          mlstm_w_out[occ], batch, seq, f"l{i}_mlstm")
        x, h = _ffn(x, h, ffn_in, ffn_out, (i, 1), f"l{i}_ffb")
    return x.reshape(batch, seq, d)
```

```python
import functools
import math
from typing import NamedTuple

import jax
import jax.numpy as jnp
from jax import lax
from jax.experimental import pallas as pl
from jax.experimental.pallas import tpu as pltpu

F32 = jnp.float32
BF16 = jnp.bfloat16

CHUNK = 64
EPS = 1e-6
ROPE_THETA = 10000.0
LOG2E = math.log2(math.e)
N_MIXERS = 4
FOX_HEADS = 32
DIFF_HEADS = 16
MLA_HEADS = 32
MLA_NOPE_DIM = 128
MLA_ROPE_DIM = 64
MLA_V_DIM = 128
MLSTM_HEADS = 8

V7X_LANES = 128
V7X_VMEM_LIMIT_BYTES = 56 * 1024 * 1024

MM_TM = 1024
MM_TN = 1024
MM_ROW_CHUNK = 256
MM_COL_CHUNK = 256
ATTN_TQ = 512
ATTN_TK = 512
SUM_ROWS = 16
MLSTM_LC = 256
NORM_TR = 256
SCAN_BLK = 512


def _tile(n, pref):
    t = min(n, pref)
    assert n % t == 0, (n, pref)
    return t


def _params(n_grid):
    return pltpu.CompilerParams(
        dimension_semantics=("arbitrary",) * n_grid,
        vmem_limit_bytes=V7X_VMEM_LIMIT_BYTES,
    )


class Rows(NamedTuple):
    xb: jax.Array
    ssq: jax.Array


def _row_stats_body(x_ref, xb_ref, ssq_ref):
    x = x_ref[...]
    xb_ref[...] = x.astype(xb_ref.dtype)
    sq = x * x
    part = sq[:, :V7X_LANES]
    for c in range(1, sq.shape[1] // V7X_LANES):
        part = part + sq[:, c * V7X_LANES:(c + 1) * V7X_LANES]
    ssq_ref[...] = part


def row_stats(x, name):
    t, d = x.shape
    tr = _tile(t, NORM_TR)
    xb, ssq = pl.pallas_call(
        _row_stats_body,
        out_shape=(jax.ShapeDtypeStruct((t, d), BF16), jax.ShapeDtypeStruct((t, V7X_LANES), F32)),
        grid=(t // tr,),
        in_specs=[pl.BlockSpec((tr, d), lambda i: (i, 0))],
        out_specs=(pl.BlockSpec((tr, d), lambda i: (i, 0)),
                   pl.BlockSpec((tr, V7X_LANES), lambda i: (i, 0))),
        compiler_params=_params(1),
        name=name,
    )(x)
    return Rows(xb, ssq)


def _mm_body(*refs, normed, n_w, n_aux, epilogue, row_chunk, col_chunk, n_acc_outs):
    a_ref = refs[0]
    refs = refs[1:]
    if normed:
        ssq_ref, refs = refs[0], refs[1:]
    w_refs = refs[:n_w]
    aux_refs = refs[n_w:n_w + n_aux]
    out_refs = refs[n_w + n_aux:]
    tm, tn = a_ref.shape[0], w_refs[0].shape[1]

    @pl.when(pl.program_id(1) == 0)
    def _():
        for o in out_refs[len(out_refs) - n_acc_outs:]:
            o[...] = jnp.zeros_like(o)

    for r0 in range(0, tm, row_chunk):
        rows = slice(r0, r0 + row_chunk)
        a = a_ref[rows, :]
        if normed:
            ssq = jnp.sum(ssq_ref[rows, :], axis=-1, keepdims=True)
            r = lax.rsqrt(ssq * (1.0 / a.shape[1]) + EPS)
        for c0 in range(0, tn, col_chunk):
            cols = slice(c0, c0 + col_chunk)

            def view(ref):
                ref = ref.at[rows] if ref.shape[0] == tm else ref
                return ref.at[:, cols] if ref.shape[1] == tn else ref

            accs = [jnp.dot(a, w[:, cols], preferred_element_type=F32) for w in w_refs]
            if normed:
                accs = [acc * r for acc in accs]
            epilogue(accs, [view(x) for x in aux_refs], [view(x) for x in out_refs])


def matmul(a, ws, w_blk_offsets, n_cols, *, tm, tn, epilogue, out_shapes, out_specs,
           aux=(), aux_specs=(), w_lead=(), row_chunk=None, col_chunk=None, n_acc_outs=0, name):
    normed = isinstance(a, Rows)
    a_arr = a.xb if normed else a
    t, k = a_arr.shape
    tm = _tile(t, tm)
    tn = _tile(n_cols, tn)
    in_specs = [pl.BlockSpec((tm, k), lambda i, j: (i, 0))]
    args = [a_arr]
    if normed:
        in_specs.append(pl.BlockSpec((tm, V7X_LANES), lambda i, j: (i, 0)))
        args.append(a.ssq)
    lead_blk = (None,) * len(w_lead)
    for off in w_blk_offsets:
        in_specs.append(pl.BlockSpec(lead_blk + (k, tn),
                                     lambda i, j, off=off: tuple(w_lead) + (0, j + off)))
    in_specs += list(aux_specs)
    body = functools.partial(_mm_body, normed=normed, n_w=len(ws), n_aux=len(aux),
                             epilogue=epilogue, row_chunk=_tile(tm, row_chunk or tm),
                             col_chunk=_tile(tn, col_chunk or tn), n_acc_outs=n_acc_outs)
    return pl.pallas_call(
        body,
        out_shape=out_shapes,
        grid=(t // tm, n_cols // tn),
        in_specs=in_specs,
        out_specs=out_specs,
        compiler_params=_params(2),
        name=name,
    )(*args, *ws, *aux)


def _ep_plain(accs, aux, outs):
    outs[0][...] = accs[0].astype(outs[0].dtype)


def _ep_swiglu(accs, aux, outs):
    g, u = accs
    outs[0][...] = (g * jax.nn.sigmoid(g) * u).astype(outs[0].dtype)


def _ep_residual(accs, aux, outs, *, alpha):
    x_out, xb_out, ssq_out = outs
    x_new = aux[0][...] + alpha * accs[0]
    x_out[...] = x_new
    xb_out[...] = x_new.astype(xb_out.dtype)
    sq = x_new * x_new
    part = sq[:, :V7X_LANES]
    for c in range(1, sq.shape[1] // V7X_LANES):
        part = part + sq[:, c * V7X_LANES:(c + 1) * V7X_LANES]
    ssq_out[...] += part


def _rope(t, cosf, sinf):
    return t * cosf + pltpu.roll(t, V7X_LANES // 2, 1) * sinf


def _ep_headnorm(accs, aux, outs, *, rope):
    acc = accs[0]
    if rope:
        cosf, sinf = aux[1][...], aux[2][...]
    for c in range(acc.shape[1] // V7X_LANES):
        sl = slice(c * V7X_LANES, (c + 1) * V7X_LANES)
        t = acc[:, sl]
        y = t * lax.rsqrt(jnp.mean(t * t, axis=-1, keepdims=True) + EPS) * aux[0][:, sl]
        if rope:
            y = _rope(y, cosf, sinf)
        outs[0][:, sl] = y.astype(outs[0].dtype)


def _ep_rownorm(accs, aux, outs):
    t = accs[0]
    y = t * lax.rsqrt(jnp.mean(t * t, axis=-1, keepdims=True) + EPS) * aux[0][...]
    outs[0][...] = y.astype(outs[0].dtype)


def _ep_mla_q(accs, aux, outs, *, n_valid):
    acc = accs[0]
    gain, cosf, sinf = aux[0][...], aux[1][...], aux[2][...]
    w = 2 * V7X_LANES
    for c in range(acc.shape[1] // w):
        t = acc[:, c * w:(c + 1) * w]
        r = lax.rsqrt(jnp.sum(t * t, axis=-1, keepdims=True) / n_valid + EPS)
        y = t * r * gain
        o = c * w
        outs[0][:, o:o + V7X_LANES] = y[:, :V7X_LANES].astype(outs[0].dtype)
        outs[0][:, o + V7X_LANES:o + w] = _rope(y[:, V7X_LANES:], cosf, sinf).astype(outs[0].dtype)


def _ep_mla_kv(accs, aux, outs, *, n_valid):
    acc = accs[0]
    kpe, gain_nope, gain_pe, cosf, sinf = (r[...] for r in aux)
    k_out, v_out = outs
    ssq_pe = jnp.sum(kpe * kpe, axis=-1, keepdims=True)
    pe_rot = _rope(kpe * gain_pe, cosf, sinf)
    w = 2 * V7X_LANES
    for c in range(acc.shape[1] // w):
        kn = acc[:, c * w:c * w + V7X_LANES]
        v = acc[:, c * w + V7X_LANES:(c + 1) * w]
        r = lax.rsqrt((jnp.sum(kn * kn, axis=-1, keepdims=True) + ssq_pe) / n_valid + EPS)
        o = c * w
        k_out[:, o:o + V7X_LANES] = (kn * r * gain_nope).astype(k_out.dtype)
        k_out[:, o + V7X_LANES:o + w] = (pe_rot * r).astype(k_out.dtype)
        v_out[:, o // 2:o // 2 + V7X_LANES] = v.astype(v_out.dtype)


def _out_tile(t, n, dtype, tm, tn):
    tm, tn = _tile(t, tm), _tile(n, tn)
    return jax.ShapeDtypeStruct((t, n), dtype), pl.BlockSpec((tm, tn), lambda i, j: (i, j))


def _col_blk(col_off, n, tn):
    tn = _tile(n, tn)
    assert col_off % tn == 0, (col_off, tn)
    return col_off // tn


def _n_rows(a):
    return (a.xb if isinstance(a, Rows) else a).shape[0]


def mm_plain(a, w, out_dtype, name, n_cols=None, col_off=0, tm=MM_TM, tn=MM_TN):
    t, n = _n_rows(a), n_cols or w.shape[1]
    shape, spec = _out_tile(t, n, out_dtype, tm, tn)
    return matmul(a, [w], [_col_blk(col_off, n, tn)], n, tm=tm, tn=tn, epilogue=_ep_plain,
                  out_shapes=shape, out_specs=spec, name=name)


def mm_swiglu(a, w_in, name, w_lead=(), tm=MM_TM, tn=MM_TN // 2):
    t, n = _n_rows(a), w_in.shape[-1] // 2
    tn = _tile(n, tn)
    shape, spec = _out_tile(t, n, BF16, tm, tn)
    return matmul(a, [w_in, w_in], [0, n // tn], n, tm=tm, tn=tn, epilogue=_ep_swiglu,
                  out_shapes=shape, out_specs=spec, w_lead=w_lead, name=name)


def mm_residual(a, w, res, alpha, name, w_lead=(), tm=MM_TM, tn=MM_TN // 2):
    t, n = _n_rows(a), w.shape[-1]
    tm_ = _tile(t, tm)
    shape, spec = _out_tile(t, n, F32, tm, tn)
    x, xb, ssq = matmul(
        a, [w], [0], n, tm=tm, tn=tn, epilogue=functools.partial(_ep_residual, alpha=alpha),
        out_shapes=(shape, jax.ShapeDtypeStruct((t, n), BF16),
                    jax.ShapeDtypeStruct((t, V7X_LANES), F32)),
        out_specs=(spec, spec, pl.BlockSpec((tm_, V7X_LANES), lambda i, j: (i, 0))),
        aux=[res], aux_specs=[spec], w_lead=w_lead, col_chunk=MM_COL_CHUNK, n_acc_outs=1, name=name)
    return x, Rows(xb, ssq)


def _rope_specs(t, seq, tm):
    tm = _tile(t, tm)
    assert seq % tm == 0
    nblk = seq // tm
    return pl.BlockSpec((tm, V7X_LANES), lambda i, j: (i % nblk, 0))


def mm_headnorm(a, w, gain_row, name, rope_tabs=None, seq=None, tm=MM_TM, tn=MM_TN):
    t, n = _n_rows(a), gain_row.shape[1]
    tn_ = _tile(n, tn)
    shape, spec = _out_tile(t, n, BF16, tm, tn)
    aux = [gain_row]
    aux_specs = [pl.BlockSpec((1, tn_), lambda i, j: (0, j))]
    if rope_tabs is not None:
        rs = _rope_specs(t, seq, tm)
        aux += list(rope_tabs)
        aux_specs += [rs, rs]
    return matmul(a, [w], [0], n, tm=tm, tn=tn,
                  epilogue=functools.partial(_ep_headnorm, rope=rope_tabs is not None),
                  out_shapes=shape, out_specs=spec, aux=aux, aux_specs=aux_specs,
                  row_chunk=MM_ROW_CHUNK, name=name)


def mm_rownorm(a, w, gain_row, name, col_off=0, tm=MM_TM):
    t, n = _n_rows(a), gain_row.shape[1]
    shape, spec = _out_tile(t, n, BF16, tm, n)
    return matmul(a, [w], [_col_blk(col_off, n, n)], n, tm=tm, tn=n, epilogue=_ep_rownorm,
                  out_shapes=shape, out_specs=spec, aux=[gain_row],
                  aux_specs=[pl.BlockSpec((1, n), lambda i, j: (0, 0))], name=name)


def _gate_scan_body(x_ref, bias_ref, mult_ref, o_ref, carry_ref, *, blk):
    @pl.when(pl.program_id(1) == 0)
    def _():
        carry_ref[...] = jnp.zeros_like(carry_ref)

    y = x_ref[...] + bias_ref[...]
    c = jnp.minimum(y, 0.0) - jnp.log1p(jnp.exp(-jnp.abs(y)))
    row = lax.broadcasted_iota(jnp.int32, c.shape, 0)
    shift = 1
    while shift < blk:
        c = c + jnp.where(row >= shift, pltpu.roll(c, shift, 0), 0.0)
        shift *= 2
    c = c + carry_ref[...]
    carry_ref[...] = c[blk - 1:blk, :]
    mult = mult_ref[...]
    o_ref[...] = jnp.where(mult != 0.0, c * mult, y)


def gate_scan(x, bias_row, mult_row, batch, seq, name):
    blk = _tile(seq, SCAN_BLK)
    nblk = seq // blk
    spec = pl.BlockSpec((blk, V7X_LANES), lambda b, s: (b * nblk + s, 0))
    row = pl.BlockSpec((1, V7X_LANES), lambda b, s: (0, 0))
    return pl.pallas_call(
        functools.partial(_gate_scan_body, blk=blk),
        out_shape=jax.ShapeDtypeStruct(x.shape, F32),
        grid=(batch, nblk),
        in_specs=[spec, row, row],
        out_specs=spec,
        scratch_shapes=[pltpu.VMEM((1, V7X_LANES), F32)],
        compiler_params=_params(2),
        name=name,
    )(x, bias_row, mult_row)


def _pad_lanes(v, width=V7X_LANES):
    return jnp.pad(v, [(0, 0)] * (v.ndim - 1) + [(0, width - v.shape[-1])])


class _Stream:
    def __init__(self, q_cols, k_cols, head):
        self.q_cols, self.k_cols, self.head = q_cols, k_cols, head


def _flash_body(*refs, streams, hp, dv, tq, tk, seq, frame_causal, decay, n_extra, finalize):
    q_ref, k_ref, v_ref = refs[:3]
    pos = 3
    if decay:
        cq_ref, ck_ref = refs[3:5]
        pos = 5
    extra = refs[pos:pos + n_extra]
    o_ref = refs[pos + n_extra]
    vt_ref = refs[pos + n_extra + 1]
    ckt_ref = refs[pos + n_extra + 2] if decay else None
    t_ref, acc_ref, m_ref = refs[-3:]
    qi = pl.program_id(2)

    @pl.when(qi == 0)
    def _():
        def fill(c, _):
            start = pl.multiple_of(c * tq, tq)
            ones_rows = (lax.broadcasted_iota(jnp.int32, (SUM_ROWS, tq), 0) == 0).astype(vt_ref.dtype)
            for e in range(hp):
                vt_ref[e, :dv, pl.ds(start, tq)] = v_ref[pl.ds(start, tq), e * dv:(e + 1) * dv].T
                vt_ref[e, dv:, pl.ds(start, tq)] = ones_rows
                if decay:
                    ck = jnp.broadcast_to(ck_ref[e, :, pl.ds(start, tq)], (V7X_LANES, tq))
                    ckt_ref[e, pl.ds(start, tq), :] = ck.T
            return 0
        lax.fori_loop(0, seq // tq, fill, 0)

    def diag_mask(d):
        key = lax.broadcasted_iota(jnp.int32, (tk, tq), 0) + d * tk
        qry = lax.broadcasted_iota(jnp.int32, (tk, tq), 1)
        if frame_causal:
            return key <= qry
        shift = CHUNK.bit_length() - 1
        return jnp.right_shift(key, shift) <= jnp.right_shift(qry, shift)

    n = len(streams)

    def scores(j, slot, mask):
        start = pl.multiple_of(j * tk, tk)
        base = slot * n
        for si, st in enumerate(streams):
            t = lax.dot_general(k_ref[pl.ds(start, tk), st.k_cols], q_ref[:, st.q_cols],
                                (((1,), (1,)), ((), ())), preferred_element_type=F32)
            if decay:
                ckb = ckt_ref[st.head, pl.ds(start, tk), :]
                t = t - jnp.concatenate([ckb] * (tq // V7X_LANES), axis=1)
            if mask is not None:
                t = jnp.where(mask, t, -jnp.inf)
            t_ref[base + si] = t

    def update(j, slot):
        start = pl.multiple_of(j * tk, tk)
        base = slot * n
        for si, st in enumerate(streams):
            vt = vt_ref[st.head, :, pl.ds(start, tk)]
            m = m_ref[si]
            cmax = jnp.max(t_ref[base + si], axis=0, keepdims=True)
            if decay:
                cq = cq_ref[st.head]
                m_new = jnp.maximum(m, cmax + cq)
                mt = m_new - cq
            else:
                m_new = jnp.maximum(m, cmax)
                mt = m_new
            alpha = jnp.exp2(m - m_new)
            p = jnp.exp2(t_ref[base + si] - mt)
            acc_ref[si] = alpha * acc_ref[si] + jnp.dot(vt, p.astype(BF16),
                                                        preferred_element_type=F32)
            m_ref[si] = m_new

    assert tq == tk
    acc_ref[...] = jnp.zeros_like(acc_ref)
    m_ref[...] = jnp.full(m_ref.shape, -jnp.inf, F32)

    mask = diag_mask(0)
    n_pairs = jnp.maximum(qi - 1, 0) // 2
    tail = 2 * n_pairs

    @pl.when(qi == 0)
    def _():
        scores(0, 0, mask)

    @pl.when(qi > 0)
    def _():
        scores(0, 0, None)

    def pair(i, _):
        j = 2 * i
        scores(j + 1, 1, None)
        update(j, 0)
        scores(j + 2, 0, None)
        update(j + 1, 1)
        return 0

    lax.fori_loop(0, n_pairs, pair, 0)

    @pl.when(qi == 0)
    def _():
        update(0, 0)

    @pl.when(qi - tail == 1)
    def _():
        scores(qi, 1, mask)
        update(tail, 0)
        update(qi, 1)

    @pl.when(qi - tail == 2)
    def _():
        scores(tail + 1, 1, None)
        update(tail, 0)
        scores(qi, 0, mask)
        update(tail + 1, 1)
        update(qi, 0)

    finalize([acc_ref[si, :dv] / acc_ref[si, dv:dv + 1] for si in range(n)], extra, o_ref)


def _fin_heads(outs, extra, o_ref):
    dv = outs[0].shape[0]
    for e, o in enumerate(outs):
        o_ref[:, e * dv:(e + 1) * dv] = o.T.astype(o_ref.dtype)


def _fin_diff(outs, extra, o_ref, *, lam_init):
    lp = extra[0][...]
    sub_gain = extra[1][...]
    lam = (jnp.exp(jnp.sum(lp[0:1] * lp[1:2], axis=-1, keepdims=True))
           - jnp.exp(jnp.sum(lp[2:3] * lp[3:4], axis=-1, keepdims=True)) + lam_init)
    dv = outs[0].shape[0]
    for e in range(len(outs) // 2):
        y = (outs[2 * e] - lam * outs[2 * e + 1]).T
        y = y * lax.rsqrt(jnp.mean(y * y, axis=-1, keepdims=True) + EPS) * sub_gain
        o_ref[:, e * dv:(e + 1) * dv] = (y * (1.0 - lam_init)).astype(o_ref.dtype)


def flash_attention(q_arr, k_arr, v_arr, *, batch, seq, heads, hp, n_maps, dk, dv,
                    q_blk_off, k_blk_off, v_blk_off, frame_causal,
                    decay_rows=None, extra=(), extra_specs=(), finalize=_fin_heads, name):
    tq = _tile(seq, ATTN_TQ)
    tk = _tile(tq, ATTN_TK)
    assert tk % CHUNK == 0 and tk % V7X_LANES == 0 and heads % hp == 0
    nq = seq // tq
    qw = hp * n_maps * dk
    vw = hp * dv
    decay = decay_rows is not None
    streams = []
    for e in range(hp):
        for g in range(n_maps):
            c = (e * n_maps + g) * dk
            streams.append(_Stream(slice(c, c + dk), slice(c, c + dk), e))
    in_specs = [
        pl.BlockSpec((tq, qw), lambda b, h, i: (b * nq + i, h + q_blk_off)),
        pl.BlockSpec((seq, qw), lambda b, h, i: (b, h + k_blk_off)),
        pl.BlockSpec((seq, vw), lambda b, h, i: (b, h + v_blk_off)),
    ]
    args = [q_arr, k_arr, v_arr]
    scratch = [pltpu.VMEM((hp, dv + SUM_ROWS, seq), BF16)]
    if decay:
        in_specs += [pl.BlockSpec((None, hp, 1, tq), lambda b, h, i: (b, h, 0, i)),
                     pl.BlockSpec((None, hp, 1, seq), lambda b, h, i: (b, h, 0, 0))]
        args += [decay_rows, decay_rows]
        scratch.append(pltpu.VMEM((hp, seq, V7X_LANES), F32))
    in_specs += list(extra_specs)
    args += list(extra)
    scratch.append(pltpu.VMEM((2 * len(streams), tk, tq), F32))
    scratch.append(pltpu.VMEM((len(streams), dv + SUM_ROWS, tq), F32))
    scratch.append(pltpu.VMEM((len(streams), 1, tq), F32))
    body = functools.partial(_flash_body, streams=streams, hp=hp, dv=dv, tq=tq, tk=tk, seq=seq,
                             frame_causal=frame_causal, decay=decay,
                             n_extra=len(extra), finalize=finalize)
    return pl.pallas_call(
        body,
        out_shape=jax.ShapeDtypeStruct((batch * seq, heads * dv), BF16),
        grid=(batch, heads // hp, nq),
        in_specs=in_specs,
        out_specs=pl.BlockSpec((tq, vw), lambda b, h, i: (b * nq + i, h)),
        scratch_shapes=scratch,
        compiler_params=_params(3),
        name=name,
    )(*args)


def _mlstm_body(q_ref, k_ref, v_ref, o_ref, ipc_ref, cumc_ref, gr_ref, gain_ref, y_ref,
                c_ref, n_ref, m_ref, cprev_ref, *, lc, heads, dk):
    h = pl.program_id(1)

    @pl.when(pl.program_id(2) == 0)
    def _():
        c_ref[...] = jnp.zeros_like(c_ref)
        n_ref[...] = jnp.zeros_like(n_ref)
        m_ref[...] = jnp.zeros_like(m_ref)
        cprev_ref[...] = jnp.zeros_like(cprev_ref)

    qscale = dk ** -0.5
    q = q_ref[...]
    k = k_ref[...]
    v = v_ref[...]
    ip_col = ipc_ref[...]
    ip_row = gr_ref[pl.ds(h, 1), :]
    cum_row = gr_ref[pl.ds(heads + h, 1), :]
    cprev = cprev_ref[...]
    m_prev = m_ref[...]
    b_col = cumc_ref[...] - cprev
    b_row = cum_row - cprev
    b_last = b_row[:, lc - 1:lc]

    row = lax.broadcasted_iota(jnp.int32, (lc, lc), 0)
    col = lax.broadcasted_iota(jnp.int32, (lc, lc), 1)
    d = jnp.where(col <= row, b_col - b_row + ip_row, -jnp.inf)
    inter = b_col + m_prev
    m_row = jnp.maximum(jnp.max(d, axis=-1, keepdims=True), inter)
    w_intra = jnp.exp(d - m_row)
    w_inter = jnp.exp(inter - m_row)

    qk = lax.dot_general(q, k, (((1,), (1,)), ((), ())), preferred_element_type=F32) * qscale
    a = w_intra * qk
    c_old = c_ref[...]
    num = (jnp.dot(a.astype(BF16), v, preferred_element_type=F32)
           + w_inter * (jnp.dot(q, c_old.astype(BF16), preferred_element_type=F32) * qscale))
    qn = jnp.sum(q.astype(F32) * n_ref[...], axis=-1, keepdims=True) * qscale
    den = jnp.sum(a, axis=-1, keepdims=True) + w_inter * qn
    hc = num / jnp.maximum(jnp.abs(den), jnp.exp(-m_row))

    g_row = b_last - b_row + ip_row
    g_col = b_last - b_col + ip_col
    m_new = jnp.maximum(b_last + m_prev, jnp.max(g_row, axis=-1, keepdims=True))
    decay = jnp.exp(b_last + m_prev - m_new)
    kw = k.astype(F32) * jnp.exp(g_col - m_new)
    c_ref[...] = decay * c_old + jnp.dot(kw.T.astype(BF16), v, preferred_element_type=F32)
    n_ref[...] = decay * n_ref[...] + jnp.sum(kw, axis=0, keepdims=True)
    m_ref[...] = m_new
    cprev_ref[...] = cum_row[:, lc - 1:lc]

    y = hc * lax.rsqrt(jnp.mean(hc * hc, axis=-1, keepdims=True) + EPS) * gain_ref[...]
    y_ref[...] = (y * jax.nn.sigmoid(o_ref[...])).astype(y_ref.dtype)


def mlstm_chunks(qkv, o, gate_cols, gate_rows, out_gain, *, batch, seq, heads, dk, dv, name):
    lc = _tile(seq, MLSTM_LC)
    nc = seq // lc
    kb, vb = heads, (2 * heads * dk) // dv
    col_spec = lambda off: pl.BlockSpec((None, None, lc, 1), lambda b, h, c: (b, h + off, c, 0))
    in_specs = [
        pl.BlockSpec((lc, dk), lambda b, h, c: (b * nc + c, h)),
        pl.BlockSpec((lc, dk), lambda b, h, c: (b * nc + c, kb + h)),
        pl.BlockSpec((lc, dv), lambda b, h, c: (b * nc + c, vb + h)),
        pl.BlockSpec((lc, dv), lambda b, h, c: (b * nc + c, h)),
        col_spec(0),
        col_spec(heads),
        pl.BlockSpec((None, 2 * heads, lc), lambda b, h, c: (b, 0, c)),
        pl.BlockSpec((None, 1, dv), lambda b, h, c: (h, 0, 0)),
    ]
    return pl.pallas_call(
        functools.partial(_mlstm_body, lc=lc, heads=heads, dk=dk),
        out_shape=jax.ShapeDtypeStruct((batch * seq, heads * dv), BF16),
        grid=(batch, heads, nc),
        in_specs=in_specs,
        out_specs=pl.BlockSpec((lc, dv), lambda b, h, c: (b * nc + c, h)),
        scratch_shapes=[pltpu.VMEM((dk, dv), F32), pltpu.VMEM((1, dk), F32),
                        pltpu.VMEM((1, 1), F32), pltpu.VMEM((1, 1), F32)],
        compiler_params=_params(3),
        name=name,
    )(qkv, qkv, qkv, o, gate_cols, gate_cols, gate_rows, out_gain.reshape(heads, 1, dv))


def _rope_tables(seq, dim):
    pos = jnp.arange(seq, dtype=F32)
    inv = ROPE_THETA ** (-jnp.arange(0, dim, 2, dtype=F32) / dim)
    ang = pos[:, None] * inv[None, :]
    return jnp.cos(ang), jnp.sin(ang)


def _ffn(x, h, w_in_all, w_out_all, lead, name):
    act = mm_swiglu(h, w_in_all, name + "_in", w_lead=lead)
    return mm_residual(act, w_out_all, x, 0.5, name + "_out", w_lead=lead)


def _gate_arrays(gates, n_rows, batch, seq):
    rows = gates.reshape(batch, seq, V7X_LANES)[:, :, :n_rows].transpose(0, 2, 1)
    return rows[..., None], rows


def _fox(x, h, w_in, q_gain, k_gain, f_bias, w_out, batch, seq, name):
    d = x.shape[1]
    hd = d // FOX_HEADS
    hp = 4
    q_scale = hd ** -0.5 * LOG2E
    gain_row = jnp.concatenate([jnp.tile(q_gain * q_scale, FOX_HEADS),
                                jnp.tile(k_gain, FOX_HEADS)])[None, :]
    wb = w_in.astype(BF16)
    qk = mm_headnorm(h, wb, gain_row.astype(F32), name + "_qk")
    v = mm_plain(h, wb, BF16, name + "_v", n_cols=d, col_off=2 * d)
    f = mm_plain(h, _pad_lanes(w_in[:, 3 * d:]).astype(BF16), F32, name + "_f", tn=V7X_LANES)
    mult = jnp.where(jnp.arange(V7X_LANES) < FOX_HEADS, LOG2E, 0.0).astype(F32)[None, :]
    cum = gate_scan(f, _pad_lanes(f_bias[None, :]), mult, batch, seq, name + "_scan")
    _, cum_rows = _gate_arrays(cum, FOX_HEADS, batch, seq)
    o = flash_attention(qk, qk, v, batch=batch, seq=seq, heads=FOX_HEADS, hp=hp, n_maps=1,
                        dk=hd, dv=hd, q_blk_off=0, k_blk_off=FOX_HEADS // hp, v_blk_off=0,
                        frame_causal=True, decay_rows=cum_rows[:, :, None, :],
                        name=name + "_attn")
    return mm_residual(o, w_out.astype(BF16), x, 1.0, name + "_out")


def _diff(x, h, w_in, q_gain, k_gain, lam_p, sub_gain, w_out, lam_init, batch, seq, name):
    d = x.shape[1]
    hd = d // (2 * DIFF_HEADS)
    dv = 2 * hd
    n_maps = 2 * DIFF_HEADS
    cos, sin = _rope_tables(seq, hd)
    tabs = (jnp.concatenate([cos, cos], -1), jnp.concatenate([-sin, sin], -1))
    q_scale = hd ** -0.5 * LOG2E
    gain_row = jnp.concatenate([jnp.tile(q_gain * q_scale, n_maps), jnp.tile(k_gain, n_maps)])[None, :]
    wb = w_in.astype(BF16)
    qk = mm_headnorm(h, wb, gain_row.astype(F32), name + "_qk", rope_tabs=tabs, seq=seq)
    v = mm_plain(h, wb, BF16, name + "_v", n_cols=DIFF_HEADS * dv, col_off=2 * d)
    const = lambda shape: pl.BlockSpec(shape, lambda b, hh, i: (0,) * len(shape))
    hp = 2
    o = flash_attention(qk, qk, v, batch=batch, seq=seq, heads=DIFF_HEADS, hp=hp, n_maps=2,
                        dk=hd, dv=dv, q_blk_off=0, k_blk_off=DIFF_HEADS // hp, v_blk_off=0,
                        frame_causal=False, extra=[lam_p.astype(F32), sub_gain[None, :].astype(F32)],
                        extra_specs=[const((4, hd)), const((1, dv))],
                        finalize=functools.partial(_fin_diff, lam_init=lam_init),
                        name=name + "_attn")
    return mm_residual(o, w_out.astype(BF16), x, 1.0, name + "_out")


def _mla_rope_layout(t):
    half = MLA_ROPE_DIM // 2
    z = jnp.zeros(t.shape[:-1] + (V7X_LANES // 2 - half,), t.dtype)
    return jnp.concatenate([t[..., :half], z, t[..., half:], z], axis=-1)


def _mla(x, h, w_in, q_lat_gain, w_q_up, kv_lat_gain, w_kv_up, q_gain, k_gain, w_out,
         batch, seq, name):
    nh, dn, dr, dv = MLA_HEADS, MLA_NOPE_DIM, MLA_ROPE_DIM, MLA_V_DIM
    q_rank, kv_rank = w_q_up.shape[0], w_kv_up.shape[0]
    cos, sin = _rope_tables(seq, dr)
    z = jnp.zeros_like(cos)
    tabs = (jnp.concatenate([cos, z, cos, z], -1), jnp.concatenate([-sin, z, sin, z], -1))

    wb = w_in.astype(BF16)
    cq = mm_rownorm(h, wb, q_lat_gain[None, :].astype(F32), name + "_cq")
    ckv = mm_rownorm(h, wb, kv_lat_gain[None, :].astype(F32), name + "_ckv", col_off=q_rank)
    kpe = mm_plain(h, _mla_rope_layout(w_in[:, q_rank + kv_rank:]).astype(BF16), F32,
                   name + "_kpe", tn=V7X_LANES)

    def pad_head(t):
        return jnp.concatenate([t[..., :dn], _mla_rope_layout(t[..., dn:])], axis=-1)

    wq = pad_head(w_q_up.reshape(q_rank, nh, dn + dr)).reshape(q_rank, nh * 2 * V7X_LANES)
    t = x.shape[0]
    tm = _tile(t, MM_TM)
    rs = _rope_specs(t, seq, tm)
    qshape, qspec = _out_tile(t, wq.shape[1], BF16, tm, MM_TN)
    row256 = pl.BlockSpec((1, 2 * V7X_LANES), lambda i, j: (0, 0))
    row128 = pl.BlockSpec((1, V7X_LANES), lambda i, j: (0, 0))
    q = matmul(cq, [wq.astype(BF16)], [0], wq.shape[1], tm=tm, tn=MM_TN,
               epilogue=functools.partial(_ep_mla_q, n_valid=dn + dr),
               out_shapes=qshape, out_specs=qspec,
               aux=[pad_head(q_gain * ((dn + dr) ** -0.5 * LOG2E))[None, :].astype(F32),
                    tabs[0], tabs[1]],
               aux_specs=[row256, rs, rs], row_chunk=MM_ROW_CHUNK, name=name + "_q")

    n_kv = w_kv_up.shape[1]
    tn = _tile(n_kv, MM_TN)
    kshape, kspec = _out_tile(t, n_kv, BF16, tm, tn)
    vshape = jax.ShapeDtypeStruct((t, n_kv // 2), BF16)
    vspec = pl.BlockSpec((tm, tn // 2), lambda i, j: (i, j))
    k, v = matmul(ckv, [w_kv_up.astype(BF16)], [0], n_kv, tm=tm, tn=tn,
                  epilogue=functools.partial(_ep_mla_kv, n_valid=dn + dr),
                  out_shapes=(kshape, vshape), out_specs=(kspec, vspec),
                  aux=[kpe, k_gain[None, :dn].astype(F32),
                       _mla_rope_layout(k_gain[dn:])[None, :].astype(F32), tabs[0], tabs[1]],
                  aux_specs=[pl.BlockSpec((tm, V7X_LANES), lambda i, j: (i, 0)), row128, row128, rs, rs],
                  row_chunk=2 * MM_ROW_CHUNK, name=name + "_kv")
    o = flash_attention(q, k, v, batch=batch, seq=seq, heads=nh, hp=4, n_maps=1,
                        dk=2 * V7X_LANES, dv=dv, q_blk_off=0, k_blk_off=0, v_blk_off=0,
                        frame_causal=False, name=name + "_attn")
    return mm_residual(o, w_out.astype(BF16), x, 1.0, name + "_out")


def _mlstm(x, h, w_in, gate_bias, out_gain, w_out, batch, seq, name):
    d = x.shape[1]
    nh = MLSTM_HEADS
    dk, dv = d // (2 * nh), d // nh
    n_qkv = 2 * nh * dk + nh * dv
    wb = w_in.astype(BF16)
    qkv = mm_plain(h, wb, BF16, name + "_qkv", n_cols=n_qkv)
    o = mm_plain(h, wb, F32, name + "_o", n_cols=d, col_off=n_qkv)
    gates = mm_plain(h, _pad_lanes(w_in[:, n_qkv + d:]).astype(BF16), F32, name + "_g", tn=V7X_LANES)
    bias = _pad_lanes(jnp.concatenate([gate_bias[0], gate_bias[1]])[None, :])
    lane = jnp.arange(V7X_LANES)
    mult = ((lane >= nh) & (lane < 2 * nh)).astype(F32)[None, :]
    scanned = gate_scan(gates, bias, mult, batch, seq, name + "_scan")
    gate_cols, gate_rows = _gate_arrays(scanned, 2 * nh, batch, seq)
    y = mlstm_chunks(qkv, o, gate_cols, gate_rows, out_gain.astype(F32), batch=batch, seq=seq,
                     heads=nh, dk=dk, dv=dv, name=name + "_cell")
    return mm_residual(y, w_out.astype(BF16), x, 1.0, name + "_out")


def kernel(x, norm_ffn, ffn_w_in, ffn_w_out, norm_mix, fox_w_in, fox_q_gain, fox_k_gain, fox_f_bias, fox_w_out, diff_w_in, diff_q_gain, diff_k_gain, diff_lambda, diff_sub_gain, diff_w_out, mla_w_in, mla_q_lat_gain, mla_w_q_up, mla_kv_lat_gain, mla_w_kv_up, mla_q_gain, mla_k_gain, mla_w_out, mlstm_w_in, mlstm_gate_bias, mlstm_out_gain, mlstm_w_out):
    batch, seq, d = x.shape
    depth = norm_mix.shape[0]
    x = x.reshape(batch * seq, d)
    ffn_in = (norm_ffn[..., None] * ffn_w_in).astype(BF16)
    ffn_out = ffn_w_out.astype(BF16)
    h = row_stats(x, "rows0")
    for i in range(depth):
        kind, occ = i % N_MIXERS, i // N_MIXERS
        gain = norm_mix[i][:, None]
        x, h = _ffn(x, h, ffn_in, ffn_out, (i, 0), f"l{i}_ffa")
        if kind == 0:
            x, h = _fox(x, h, gain * fox_w_in[occ], fox_q_gain[occ], fox_k_gain[occ],
                        fox_f_bias[occ], fox_w_out[occ], batch, seq, f"l{i}_fox")
        elif kind == 1:
            lam_init = 0.8 - 0.6 * math.exp(-0.3 * i)
            x, h = _diff(x, h, gain * diff_w_in[occ], diff_q_gain[occ], diff_k_gain[occ],
                         diff_lambda[occ], diff_sub_gain[occ], diff_w_out[occ], lam_init,
                         batch, seq, f"l{i}_diff")
        elif kind == 2:
            x, h = _mla(x, h, gain * mla_w_in[occ], mla_q_lat_gain[occ], mla_w_q_up[occ],
                        mla_kv_lat_gain[occ], mla_w_kv_up[occ], mla_q_gain[occ], mla_k_gain[occ],
                        mla_w_out[occ], batch, seq, f"l{i}_mla")
        else:
            x, h = _mlstm(x, h, gain * mlstm_w_in[occ], mlstm_gate_bias[occ], mlstm_out_gain[occ],
                          mlstm_w_out[occ], batch, seq, f"l{i}_mlstm")
        x, h = _ffn(x, h, ffn_in, ffn_out, (i, 1), f"l{i}_ffb")
    return x.reshape(batch, seq, d)
```

```python
import functools
import math
from typing import NamedTuple

import jax
import jax.numpy as jnp
from jax import lax
from jax.experimental import pallas as pl
from jax.experimental.pallas import tpu as pltpu

F32 = jnp.float32
BF16 = jnp.bfloat16

CHUNK = 64
EPS = 1e-6
ROPE_THETA = 10000.0
LOG2E = math.log2(math.e)
N_MIXERS = 4
FOX_HEADS = 32
DIFF_HEADS = 16
MLA_HEADS = 32
MLA_NOPE_DIM = 128
MLA_ROPE_DIM = 64
MLA_V_DIM = 128
MLSTM_HEADS = 8

V7X_LANES = 128
V7X_VMEM_LIMIT_BYTES = 56 * 1024 * 1024

MM_TM = 1024
MM_TN = 1024
MM_ROW_CHUNK = 256
MM_COL_CHUNK = 256
SWIGLU_GROUP = 256
ATTN_TQ = 512
ATTN_TK = 512
SUM_ROWS = 16
MLSTM_LC = 256
NORM_TR = 256
SCAN_BLK = 512


def _tile(n, pref):
    t = min(n, pref)
    assert n % t == 0, (n, pref)
    return t


def _params(n_grid):
    return pltpu.CompilerParams(
        dimension_semantics=("arbitrary",) * n_grid,
        vmem_limit_bytes=V7X_VMEM_LIMIT_BYTES,
    )


class Rows(NamedTuple):
    xb: jax.Array
    ssq: jax.Array


def _row_stats_body(x_ref, xb_ref, ssq_ref):
    x = x_ref[...]
    xb_ref[...] = x.astype(xb_ref.dtype)
    sq = x * x
    part = sq[:, :V7X_LANES]
    for c in range(1, sq.shape[1] // V7X_LANES):
        part = part + sq[:, c * V7X_LANES:(c + 1) * V7X_LANES]
    ssq_ref[...] = part


def row_stats(x, name):
    t, d = x.shape
    tr = _tile(t, NORM_TR)
    xb, ssq = pl.pallas_call(
        _row_stats_body,
        out_shape=(jax.ShapeDtypeStruct((t, d), BF16), jax.ShapeDtypeStruct((t, V7X_LANES), F32)),
        grid=(t // tr,),
        in_specs=[pl.BlockSpec((tr, d), lambda i: (i, 0))],
        out_specs=(pl.BlockSpec((tr, d), lambda i: (i, 0)),
                   pl.BlockSpec((tr, V7X_LANES), lambda i: (i, 0))),
        compiler_params=_params(1),
        name=name,
    )(x)
    return Rows(xb, ssq)


def _mm_body(*refs, normed, n_w, n_aux, epilogue, row_chunk, col_chunk, n_acc_outs):
    a_ref = refs[0]
    refs = refs[1:]
    if normed:
        ssq_ref, refs = refs[0], refs[1:]
    w_refs = refs[:n_w]
    aux_refs = refs[n_w:n_w + n_aux]
    out_refs = refs[n_w + n_aux:]
    tm, tn = a_ref.shape[0], w_refs[0].shape[1]

    @pl.when(pl.program_id(1) == 0)
    def _():
        for o in out_refs[len(out_refs) - n_acc_outs:]:
            o[...] = jnp.zeros_like(o)

    for r0 in range(0, tm, row_chunk):
        rows = slice(r0, r0 + row_chunk)
        a = a_ref[rows, :]
        if normed:
            ssq = jnp.sum(ssq_ref[rows, :], axis=-1, keepdims=True)
            r = lax.rsqrt(ssq * (1.0 / a.shape[1]) + EPS)
        for c0 in range(0, tn, col_chunk):
            cols = slice(c0, c0 + col_chunk)

            def view(ref):
                ref = ref.at[rows] if ref.shape[0] == tm else ref
                return ref.at[:, cols] if ref.shape[1] == tn and col_chunk < tn else ref

            accs = [jnp.dot(a, w[:, cols], preferred_element_type=F32) for w in w_refs]
            views = ([view(x) for x in aux_refs], [view(x) for x in out_refs])
            if normed and getattr(epilogue, "applies_row_scale", False):
                epilogue(accs, *views, r)
            else:
                if normed:
                    accs = [acc * r for acc in accs]
                epilogue(accs, *views)


def matmul(a, ws, w_blk_offsets, n_cols, *, tm, tn, epilogue, out_shapes, out_specs,
           aux=(), aux_specs=(), w_lead=(), row_chunk=None, col_chunk=None, n_acc_outs=0,
           w_width=None, name):
    normed = isinstance(a, Rows)
    a_arr = a.xb if normed else a
    t, k = a_arr.shape
    tm = _tile(t, tm)
    tn = _tile(n_cols, tn)
    in_specs = [pl.BlockSpec((tm, k), lambda i, j: (i, 0))]
    args = [a_arr]
    if normed:
        in_specs.append(pl.BlockSpec((tm, V7X_LANES), lambda i, j: (i, 0)))
        args.append(a.ssq)
    lead_blk = (None,) * len(w_lead)
    for off in w_blk_offsets:
        in_specs.append(pl.BlockSpec(lead_blk + (k, w_width or tn),
                                     lambda i, j, off=off: tuple(w_lead) + (0, j + off)))
    in_specs += list(aux_specs)
    body = functools.partial(_mm_body, normed=normed, n_w=len(ws), n_aux=len(aux),
                             epilogue=epilogue, row_chunk=_tile(tm, row_chunk or tm),
                             col_chunk=_tile(w_width or tn, col_chunk or w_width or tn),
                             n_acc_outs=n_acc_outs)
    return pl.pallas_call(
        body,
        out_shape=out_shapes,
        grid=(t // tm, n_cols // tn),
        in_specs=in_specs,
        out_specs=out_specs,
        compiler_params=_params(2),
        name=name,
    )(*args, *ws, *aux)


def _ep_plain(accs, aux, outs):
    outs[0][...] = accs[0].astype(outs[0].dtype)


def _ep_swiglu(accs, aux, outs, r):
    acc = accs[0]
    w = SWIGLU_GROUP
    r_exp = r * (-LOG2E)
    r_sq = r * r
    for c in range(acc.shape[1] // (2 * w)):
        g = acc[:, 2 * c * w:(2 * c + 1) * w]
        u = acc[:, (2 * c + 1) * w:(2 * c + 2) * w]
        y = (g * u) * r_sq / (1.0 + jnp.exp2(g * r_exp))
        outs[0][:, c * w:(c + 1) * w] = y.astype(outs[0].dtype)


_ep_swiglu.applies_row_scale = True


def _interleave_gate_up(w_in):
    *lead, k, n2 = w_in.shape
    f = n2 // 2
    g = _tile(f, SWIGLU_GROUP)
    w = w_in.reshape(*lead, k, 2, f // g, g)
    return jnp.swapaxes(w, -3, -2).reshape(*lead, k, n2)


def _ep_residual(accs, aux, outs, *, alpha):
    x_out, xb_out, ssq_out = outs
    x_new = aux[0][...] + alpha * accs[0]
    x_out[...] = x_new
    xb_out[...] = x_new.astype(xb_out.dtype)
    sq = x_new * x_new
    part = sq[:, :V7X_LANES]
    for c in range(1, sq.shape[1] // V7X_LANES):
        part = part + sq[:, c * V7X_LANES:(c + 1) * V7X_LANES]
    ssq_out[...] += part


def _rope(t, cosf, sinf):
    return t * cosf + pltpu.roll(t, V7X_LANES // 2, 1) * sinf


def _ep_headnorm(accs, aux, outs, *, rope):
    acc = accs[0]
    if rope:
        cosf, sinf = aux[1][...], aux[2][...]
    for c in range(acc.shape[1] // V7X_LANES):
        sl = slice(c * V7X_LANES, (c + 1) * V7X_LANES)
        t = acc[:, sl]
        y = t * lax.rsqrt(jnp.mean(t * t, axis=-1, keepdims=True) + EPS) * aux[0][:, sl]
        if rope:
            y = _rope(y, cosf, sinf)
        outs[0][:, sl] = y.astype(outs[0].dtype)


def _ep_rownorm(accs, aux, outs):
    t = accs[0]
    y = t * lax.rsqrt(jnp.mean(t * t, axis=-1, keepdims=True) + EPS) * aux[0][...]
    outs[0][...] = y.astype(outs[0].dtype)


def _ep_mla_q(accs, aux, outs, *, n_valid):
    acc = accs[0]
    gain, cosf, sinf = aux[0][...], aux[1][...], aux[2][...]
    w = 2 * V7X_LANES
    for c in range(acc.shape[1] // w):
        t = acc[:, c * w:(c + 1) * w]
        r = lax.rsqrt(jnp.sum(t * t, axis=-1, keepdims=True) / n_valid + EPS)
        y = t * r * gain
        o = c * w
        outs[0][:, o:o + V7X_LANES] = y[:, :V7X_LANES].astype(outs[0].dtype)
        outs[0][:, o + V7X_LANES:o + w] = _rope(y[:, V7X_LANES:], cosf, sinf).astype(outs[0].dtype)


def _ep_mla_kv(accs, aux, outs, *, n_valid):
    acc = accs[0]
    kpe, gain_nope, gain_pe, cosf, sinf = (r[...] for r in aux)
    k_out, v_out = outs
    ssq_pe = jnp.sum(kpe * kpe, axis=-1, keepdims=True)
    pe_rot = _rope(kpe * gain_pe, cosf, sinf)
    w = 2 * V7X_LANES
    for c in range(acc.shape[1] // w):
        kn = acc[:, c * w:c * w + V7X_LANES]
        v = acc[:, c * w + V7X_LANES:(c + 1) * w]
        r = lax.rsqrt((jnp.sum(kn * kn, axis=-1, keepdims=True) + ssq_pe) / n_valid + EPS)
        o = c * w
        k_out[:, o:o + V7X_LANES] = (kn * r * gain_nope).astype(k_out.dtype)
        k_out[:, o + V7X_LANES:o + w] = (pe_rot * r).astype(k_out.dtype)
        v_out[:, o // 2:o // 2 + V7X_LANES] = v.astype(v_out.dtype)


def _out_tile(t, n, dtype, tm, tn):
    tm, tn = _tile(t, tm), _tile(n, tn)
    return jax.ShapeDtypeStruct((t, n), dtype), pl.BlockSpec((tm, tn), lambda i, j: (i, j))


def _col_blk(col_off, n, tn):
    tn = _tile(n, tn)
    assert col_off % tn == 0, (col_off, tn)
    return col_off // tn


def _n_rows(a):
    return (a.xb if isinstance(a, Rows) else a).shape[0]


def mm_plain(a, w, out_dtype, name, n_cols=None, col_off=0, tm=MM_TM, tn=MM_TN):
    t, n = _n_rows(a), n_cols or w.shape[1]
    shape, spec = _out_tile(t, n, out_dtype, tm, tn)
    return matmul(a, [w], [_col_blk(col_off, n, tn)], n, tm=tm, tn=tn, epilogue=_ep_plain,
                  out_shapes=shape, out_specs=spec, name=name)


def mm_swiglu(a, w_in, name, w_lead=(), tm=MM_TM, tn=MM_TN // 2):
    t, n = _n_rows(a), w_in.shape[-1] // 2
    tn = _tile(n, tn)
    assert tn % SWIGLU_GROUP == 0
    shape, spec = _out_tile(t, n, BF16, tm, tn)
    return matmul(a, [w_in], [0], n, tm=tm, tn=tn, epilogue=_ep_swiglu,
                  out_shapes=shape, out_specs=spec, w_lead=w_lead, w_width=2 * tn, name=name)


def mm_residual(a, w, res, alpha, name, w_lead=(), tm=MM_TM, tn=MM_TN // 2):
    t, n = _n_rows(a), w.shape[-1]
    tm_ = _tile(t, tm)
    shape, spec = _out_tile(t, n, F32, tm, tn)
    x, xb, ssq = matmul(
        a, [w], [0], n, tm=tm, tn=tn, epilogue=functools.partial(_ep_residual, alpha=alpha),
        out_shapes=(shape, jax.ShapeDtypeStruct((t, n), BF16),
                    jax.ShapeDtypeStruct((t, V7X_LANES), F32)),
        out_specs=(spec, spec, pl.BlockSpec((tm_, V7X_LANES), lambda i, j: (i, 0))),
        aux=[res], aux_specs=[spec], w_lead=w_lead, col_chunk=MM_COL_CHUNK, n_acc_outs=1, name=name)
    return x, Rows(xb, ssq)


def _rope_specs(t, seq, tm):
    tm = _tile(t, tm)
    assert seq % tm == 0
    nblk = seq // tm
    return pl.BlockSpec((tm, V7X_LANES), lambda i, j: (i % nblk, 0))


def mm_headnorm(a, w, gain_row, name, rope_tabs=None, seq=None, tm=MM_TM, tn=MM_TN):
    t, n = _n_rows(a), gain_row.shape[1]
    tn_ = _tile(n, tn)
    shape, spec = _out_tile(t, n, BF16, tm, tn)
    aux = [gain_row]
    aux_specs = [pl.BlockSpec((1, tn_), lambda i, j: (0, j))]
    if rope_tabs is not None:
        rs = _rope_specs(t, seq, tm)
        aux += list(rope_tabs)
        aux_specs += [rs, rs]
    return matmul(a, [w], [0], n, tm=tm, tn=tn,
                  epilogue=functools.partial(_ep_headnorm, rope=rope_tabs is not None),
                  out_shapes=shape, out_specs=spec, aux=aux, aux_specs=aux_specs,
                  row_chunk=MM_ROW_CHUNK, name=name)


def mm_rownorm(a, w, gain_row, name, col_off=0, tm=MM_TM):
    t, n = _n_rows(a), gain_row.shape[1]
    shape, spec = _out_tile(t, n, BF16, tm, n)
    return matmul(a, [w], [_col_blk(col_off, n, n)], n, tm=tm, tn=n, epilogue=_ep_rownorm,
                  out_shapes=shape, out_specs=spec, aux=[gain_row],
                  aux_specs=[pl.BlockSpec((1, n), lambda i, j: (0, 0))], name=name)


def _gate_scan_body(x_ref, bias_ref, mult_ref, o_ref, carry_ref, *, blk):
    @pl.when(pl.program_id(1) == 0)
    def _():
        carry_ref[...] = jnp.zeros_like(carry_ref)

    y = x_ref[...] + bias_ref[...]
    c = jnp.minimum(y, 0.0) - jnp.log1p(jnp.exp(-jnp.abs(y)))
    row = lax.broadcasted_iota(jnp.int32, c.shape, 0)
    shift = 1
    while shift < blk:
        c = c + jnp.where(row >= shift, pltpu.roll(c, shift, 0), 0.0)
        shift *= 2
    c = c + carry_ref[...]
    carry_ref[...] = c[blk - 1:blk, :]
    mult = mult_ref[...]
    o_ref[...] = jnp.where(mult != 0.0, c * mult, y)


def gate_scan(x, bias_row, mult_row, batch, seq, name):
    blk = _tile(seq, SCAN_BLK)
    nblk = seq // blk
    spec = pl.BlockSpec((blk, V7X_LANES), lambda b, s: (b * nblk + s, 0))
    row = pl.BlockSpec((1, V7X_LANES), lambda b, s: (0, 0))
    return pl.pallas_call(
        functools.partial(_gate_scan_body, blk=blk),
        out_shape=jax.ShapeDtypeStruct(x.shape, F32),
        grid=(batch, nblk),
        in_specs=[spec, row, row],
        out_specs=spec,
        scratch_shapes=[pltpu.VMEM((1, V7X_LANES), F32)],
        compiler_params=_params(2),
        name=name,
    )(x, bias_row, mult_row)


def _pad_lanes(v, width=V7X_LANES):
    return jnp.pad(v, [(0, 0)] * (v.ndim - 1) + [(0, width - v.shape[-1])])


class _Stream:
    def __init__(self, q_cols, k_cols, head):
        self.q_cols, self.k_cols, self.head = q_cols, k_cols, head


def _flash_body(*refs, streams, hp, dv, tq, tk, seq, frame_causal, decay, n_extra, finalize):
    q_ref, k_ref, v_ref = refs[:3]
    pos = 3
    if decay:
        cq_ref, ck_ref = refs[3:5]
        pos = 5
    extra = refs[pos:pos + n_extra]
    o_ref = refs[pos + n_extra]
    vt_ref = refs[pos + n_extra + 1]
    ckt_ref = refs[pos + n_extra + 2] if decay else None
    t_ref, acc_ref, m_ref = refs[-3:]
    qi = pl.program_id(2)

    @pl.when(qi == 0)
    def _():
        def fill(c, _):
            start = pl.multiple_of(c * tq, tq)
            ones_rows = (lax.broadcasted_iota(jnp.int32, (SUM_ROWS, tq), 0) == 0).astype(vt_ref.dtype)
            for e in range(hp):
                vt_ref[e, :dv, pl.ds(start, tq)] = v_ref[pl.ds(start, tq), e * dv:(e + 1) * dv].T
                vt_ref[e, dv:, pl.ds(start, tq)] = ones_rows
                if decay:
                    ck = jnp.broadcast_to(ck_ref[e, :, pl.ds(start, tq)], (V7X_LANES, tq))
                    ckt_ref[e, pl.ds(start, tq), :] = ck.T
            return 0
        lax.fori_loop(0, seq // tq, fill, 0)

    def diag_mask(d):
        key = lax.broadcasted_iota(jnp.int32, (tk, tq), 0) + d * tk
        qry = lax.broadcasted_iota(jnp.int32, (tk, tq), 1)
        if frame_causal:
            return key <= qry
        shift = CHUNK.bit_length() - 1
        return jnp.right_shift(key, shift) <= jnp.right_shift(qry, shift)

    n = len(streams)

    def scores(j, slot, mask):
        start = pl.multiple_of(j * tk, tk)
        base = slot * n
        for si, st in enumerate(streams):
            t = lax.dot_general(k_ref[pl.ds(start, tk), st.k_cols], q_ref[:, st.q_cols],
                                (((1,), (1,)), ((), ())), preferred_element_type=F32)
            if decay:
                ckb = ckt_ref[st.head, pl.ds(start, tk), :]
                t = t - jnp.concatenate([ckb] * (tq // V7X_LANES), axis=1)
            if mask is not None:
                t = jnp.where(mask, t, -jnp.inf)
            t_ref[base + si] = t

    def update(j, slot):
        start = pl.multiple_of(j * tk, tk)
        base = slot * n
        for si, st in enumerate(streams):
            vt = vt_ref[st.head, :, pl.ds(start, tk)]
            m = m_ref[si]
            cmax = jnp.max(t_ref[base + si], axis=0, keepdims=True)
            if decay:
                cq = cq_ref[st.head]
                m_new = jnp.maximum(m, cmax + cq)
                mt = m_new - cq
            else:
                m_new = jnp.maximum(m, cmax)
                mt = m_new
            alpha = jnp.exp2(m - m_new)
            p = jnp.exp2(t_ref[base + si] - mt)
            acc_ref[si] = alpha * acc_ref[si] + jnp.dot(vt, p.astype(BF16),
                                                        preferred_element_type=F32)
            m_ref[si] = m_new

    assert tq == tk
    acc_ref[...] = jnp.zeros_like(acc_ref)
    m_ref[...] = jnp.full(m_ref.shape, -jnp.inf, F32)

    mask = diag_mask(0)
    n_pairs = jnp.maximum(qi - 1, 0) // 2
    tail = 2 * n_pairs

    @pl.when(qi == 0)
    def _():
        scores(0, 0, mask)

    @pl.when(qi > 0)
    def _():
        scores(0, 0, None)

    def pair(i, _):
        j = 2 * i
        scores(j + 1, 1, None)
        update(j, 0)
        scores(j + 2, 0, None)
        update(j + 1, 1)
        return 0

    lax.fori_loop(0, n_pairs, pair, 0)

    @pl.when(qi == 0)
    def _():
        update(0, 0)

    @pl.when(qi - tail == 1)
    def _():
        scores(qi, 1, mask)
        update(tail, 0)
        update(qi, 1)

    @pl.when(qi - tail == 2)
    def _():
        scores(tail + 1, 1, None)
        update(tail, 0)
        scores(qi, 0, mask)
        update(tail + 1, 1)
        update(qi, 0)

    finalize([acc_ref[si, :dv] / acc_ref[si, dv:dv + 1] for si in range(n)], extra, o_ref)


def _fin_heads(outs, extra, o_ref):
    dv = outs[0].shape[0]
    for e, o in enumerate(outs):
        o_ref[:, e * dv:(e + 1) * dv] = o.T.astype(o_ref.dtype)


def _fin_diff(outs, extra, o_ref, *, lam_init):
    lp = extra[0][...]
    sub_gain = extra[1][...]
    lam = (jnp.exp(jnp.sum(lp[0:1] * lp[1:2], axis=-1, keepdims=True))
           - jnp.exp(jnp.sum(lp[2:3] * lp[3:4], axis=-1, keepdims=True)) + lam_init)
    dv = outs[0].shape[0]
    for e in range(len(outs) // 2):
        y = (outs[2 * e] - lam * outs[2 * e + 1]).T
        y = y * lax.rsqrt(jnp.mean(y * y, axis=-1, keepdims=True) + EPS) * sub_gain
        o_ref[:, e * dv:(e + 1) * dv] = (y * (1.0 - lam_init)).astype(o_ref.dtype)


def flash_attention(q_arr, k_arr, v_arr, *, batch, seq, heads, hp, n_maps, dk, dv,
                    q_blk_off, k_blk_off, v_blk_off, frame_causal,
                    decay_rows=None, extra=(), extra_specs=(), finalize=_fin_heads, name):
    tq = _tile(seq, ATTN_TQ)
    tk = _tile(tq, ATTN_TK)
    assert tk % CHUNK == 0 and tk % V7X_LANES == 0 and heads % hp == 0
    nq = seq // tq
    qw = hp * n_maps * dk
    vw = hp * dv
    decay = decay_rows is not None
    streams = []
    for e in range(hp):
        for g in range(n_maps):
            c = (e * n_maps + g) * dk
            streams.append(_Stream(slice(c, c + dk), slice(c, c + dk), e))
    in_specs = [
        pl.BlockSpec((tq, qw), lambda b, h, i: (b * nq + i, h + q_blk_off)),
        pl.BlockSpec((seq, qw), lambda b, h, i: (b, h + k_blk_off)),
        pl.BlockSpec((seq, vw), lambda b, h, i: (b, h + v_blk_off)),
    ]
    args = [q_arr, k_arr, v_arr]
    scratch = [pltpu.VMEM((hp, dv + SUM_ROWS, seq), BF16)]
    if decay:
        in_specs += [pl.BlockSpec((None, hp, 1, tq), lambda b, h, i: (b, h, 0, i)),
                     pl.BlockSpec((None, hp, 1, seq), lambda b, h, i: (b, h, 0, 0))]
        args += [decay_rows, decay_rows]
        scratch.append(pltpu.VMEM((hp, seq, V7X_LANES), F32))
    in_specs += list(extra_specs)
    args += list(extra)
    scratch.append(pltpu.VMEM((2 * len(streams), tk, tq), F32))
    scratch.append(pltpu.VMEM((len(streams), dv + SUM_ROWS, tq), F32))
    scratch.append(pltpu.VMEM((len(streams), 1, tq), F32))
    body = functools.partial(_flash_body, streams=streams, hp=hp, dv=dv, tq=tq, tk=tk, seq=seq,
                             frame_causal=frame_causal, decay=decay,
                             n_extra=len(extra), finalize=finalize)
    return pl.pallas_call(
        body,
        out_shape=jax.ShapeDtypeStruct((batch * seq, heads * dv), BF16),
        grid=(batch, heads // hp, nq),
        in_specs=in_specs,
        out_specs=pl.BlockSpec((tq, vw), lambda b, h, i: (b * nq + i, h)),
        scratch_shapes=scratch,
        compiler_params=_params(3),
        name=name,
    )(*args)


def _mlstm_body(q_ref, k_ref, v_ref, o_ref, ipc_ref, cumc_ref, gr_ref, gain_ref, y_ref,
                c_ref, n_ref, m_ref, cprev_ref, *, lc, heads, dk):
    h = pl.program_id(1)

    @pl.when(pl.program_id(2) == 0)
    def _():
        c_ref[...] = jnp.zeros_like(c_ref)
        n_ref[...] = jnp.zeros_like(n_ref)
        m_ref[...] = jnp.zeros_like(m_ref)
        cprev_ref[...] = jnp.zeros_like(cprev_ref)

    qscale = dk ** -0.5
    q = q_ref[...]
    k = k_ref[...]
    v = v_ref[...]
    ip_col = ipc_ref[...]
    ip_row = gr_ref[pl.ds(h, 1), :]
    cum_row = gr_ref[pl.ds(heads + h, 1), :]
    cprev = cprev_ref[...]
    m_prev = m_ref[...]
    b_col = cumc_ref[...] - cprev
    b_row = cum_row - cprev
    b_last = b_row[:, lc - 1:lc]

    row = lax.broadcasted_iota(jnp.int32, (lc, lc), 0)
    col = lax.broadcasted_iota(jnp.int32, (lc, lc), 1)
    d = jnp.where(col <= row, b_col - b_row + ip_row, -jnp.inf)
    inter = b_col + m_prev
    m_row = jnp.maximum(jnp.max(d, axis=-1, keepdims=True), inter)
    w_intra = jnp.exp(d - m_row)
    w_inter = jnp.exp(inter - m_row)

    qk = lax.dot_general(q, k, (((1,), (1,)), ((), ())), preferred_element_type=F32) * qscale
    a = w_intra * qk
    c_old = c_ref[...]
    num = (jnp.dot(a.astype(BF16), v, preferred_element_type=F32)
           + w_inter * (jnp.dot(q, c_old.astype(BF16), preferred_element_type=F32) * qscale))
    qn = jnp.sum(q.astype(F32) * n_ref[...], axis=-1, keepdims=True) * qscale
    den = jnp.sum(a, axis=-1, keepdims=True) + w_inter * qn
    hc = num / jnp.maximum(jnp.abs(den), jnp.exp(-m_row))

    g_row = b_last - b_row + ip_row
    g_col = b_last - b_col + ip_col
    m_new = jnp.maximum(b_last + m_prev, jnp.max(g_row, axis=-1, keepdims=True))
    decay = jnp.exp(b_last + m_prev - m_new)
    kw = k.astype(F32) * jnp.exp(g_col - m_new)
    c_ref[...] = decay * c_old + jnp.dot(kw.T.astype(BF16), v, preferred_element_type=F32)
    n_ref[...] = decay * n_ref[...] + jnp.sum(kw, axis=0, keepdims=True)
    m_ref[...] = m_new
    cprev_ref[...] = cum_row[:, lc - 1:lc]

    y = hc * lax.rsqrt(jnp.mean(hc * hc, axis=-1, keepdims=True) + EPS) * gain_ref[...]
    y_ref[...] = (y * jax.nn.sigmoid(o_ref[...])).astype(y_ref.dtype)


def mlstm_chunks(qkv, o, gate_cols, gate_rows, out_gain, *, batch, seq, heads, dk, dv, name):
    lc = _tile(seq, MLSTM_LC)
    nc = seq // lc
    kb, vb = heads, (2 * heads * dk) // dv
    col_spec = lambda off: pl.BlockSpec((None, None, lc, 1), lambda b, h, c: (b, h + off, c, 0))
    in_specs = [
        pl.BlockSpec((lc, dk), lambda b, h, c: (b * nc + c, h)),
        pl.BlockSpec((lc, dk), lambda b, h, c: (b * nc + c, kb + h)),
        pl.BlockSpec((lc, dv), lambda b, h, c: (b * nc + c, vb + h)),
        pl.BlockSpec((lc, dv), lambda b, h, c: (b * nc + c, h)),
        col_spec(0),
        col_spec(heads),
        pl.BlockSpec((None, 2 * heads, lc), lambda b, h, c: (b, 0, c)),
        pl.BlockSpec((None, 1, dv), lambda b, h, c: (h, 0, 0)),
    ]
    return pl.pallas_call(
        functools.partial(_mlstm_body, lc=lc, heads=heads, dk=dk),
        out_shape=jax.ShapeDtypeStruct((batch * seq, heads * dv), BF16),
        grid=(batch, heads, nc),
        in_specs=in_specs,
        out_specs=pl.BlockSpec((lc, dv), lambda b, h, c: (b * nc + c, h)),
        scratch_shapes=[pltpu.VMEM((dk, dv), F32), pltpu.VMEM((1, dk), F32),
                        pltpu.VMEM((1, 1), F32), pltpu.VMEM((1, 1), F32)],
        compiler_params=_params(3),
        name=name,
    )(qkv, qkv, qkv, o, gate_cols, gate_cols, gate_rows, out_gain.reshape(heads, 1, dv))


def _rope_tables(seq, dim):
    pos = jnp.arange(seq, dtype=F32)
    inv = ROPE_THETA ** (-jnp.arange(0, dim, 2, dtype=F32) / dim)
    ang = pos[:, None] * inv[None, :]
    return jnp.cos(ang), jnp.sin(ang)


def _ffn(x, h, w_in_all, w_out_all, lead, name):
    act = mm_swiglu(h, w_in_all, name + "_in", w_lead=lead)
    return mm_residual(act, w_out_all, x, 0.5, name + "_out", w_lead=lead)


def _gate_arrays(gates, n_rows, batch, seq):
    rows = gates.reshape(batch, seq, V7X_LANES)[:, :, :n_rows].transpose(0, 2, 1)
    return rows[..., None], rows


def _fox(x, h, w_in, q_gain, k_gain, f_bias, w_out, batch, seq, name):
    d = x.shape[1]
    hd = d // FOX_HEADS
    hp = 4
    q_scale = hd ** -0.5 * LOG2E
    gain_row = jnp.concatenate([jnp.tile(q_gain * q_scale, FOX_HEADS),
                                jnp.tile(k_gain, FOX_HEADS)])[None, :]
    wb = w_in.astype(BF16)
    qk = mm_headnorm(h, wb, gain_row.astype(F32), name + "_qk")
    v = mm_plain(h, wb, BF16, name + "_v", n_cols=d, col_off=2 * d)
    f = mm_plain(h, _pad_lanes(w_in[:, 3 * d:]).astype(BF16), F32, name + "_f", tn=V7X_LANES)
    mult = jnp.where(jnp.arange(V7X_LANES) < FOX_HEADS, LOG2E, 0.0).astype(F32)[None, :]
    cum = gate_scan(f, _pad_lanes(f_bias[None, :]), mult, batch, seq, name + "_scan")
    _, cum_rows = _gate_arrays(cum, FOX_HEADS, batch, seq)
    o = flash_attention(qk, qk, v, batch=batch, seq=seq, heads=FOX_HEADS, hp=hp, n_maps=1,
                        dk=hd, dv=hd, q_blk_off=0, k_blk_off=FOX_HEADS // hp, v_blk_off=0,
                        frame_causal=True, decay_rows=cum_rows[:, :, None, :],
                        name=name + "_attn")
    return mm_residual(o, w_out.astype(BF16), x, 1.0, name + "_out")


def _diff(x, h, w_in, q_gain, k_gain, lam_p, sub_gain, w_out, lam_init, batch, seq, name):
    d = x.shape[1]
    hd = d // (2 * DIFF_HEADS)
    dv = 2 * hd
    n_maps = 2 * DIFF_HEADS
    cos, sin = _rope_tables(seq, hd)
    tabs = (jnp.concatenate([cos, cos], -1), jnp.concatenate([-sin, sin], -1))
    q_scale = hd ** -0.5 * LOG2E
    gain_row = jnp.concatenate([jnp.tile(q_gain * q_scale, n_maps), jnp.tile(k_gain, n_maps)])[None, :]
    wb = w_in.astype(BF16)
    qk = mm_headnorm(h, wb, gain_row.astype(F32), name + "_qk", rope_tabs=tabs, seq=seq)
    v = mm_plain(h, wb, BF16, name + "_v", n_cols=DIFF_HEADS * dv, col_off=2 * d)
    const = lambda shape: pl.BlockSpec(shape, lambda b, hh, i: (0,) * len(shape))
    hp = 2
    o = flash_attention(qk, qk, v, batch=batch, seq=seq, heads=DIFF_HEADS, hp=hp, n_maps=2,
                        dk=hd, dv=dv, q_blk_off=0, k_blk_off=DIFF_HEADS // hp, v_blk_off=0,
                        frame_causal=False, extra=[lam_p.astype(F32), sub_gain[None, :].astype(F32)],
                        extra_specs=[const((4, hd)), const((1, dv))],
                        finalize=functools.partial(_fin_diff, lam_init=lam_init),
                        name=name + "_attn")
    return mm_residual(o, w_out.astype(BF16), x, 1.0, name + "_out")


def _mla_rope_layout(t):
    half = MLA_ROPE_DIM // 2
    z = jnp.zeros(t.shape[:-1] + (V7X_LANES // 2 - half,), t.dtype)
    return jnp.concatenate([t[..., :half], z, t[..., half:], z], axis=-1)


def _mla(x, h, w_in, q_lat_gain, w_q_up, kv_lat_gain, w_kv_up, q_gain, k_gain, w_out,
         batch, seq, name):
    nh, dn, dr, dv = MLA_HEADS, MLA_NOPE_DIM, MLA_ROPE_DIM, MLA_V_DIM
    q_rank, kv_rank = w_q_up.shape[0], w_kv_up.shape[0]
    cos, sin = _rope_tables(seq, dr)
    z = jnp.zeros_like(cos)
    tabs = (jnp.concatenate([cos, z, cos, z], -1), jnp.concatenate([-sin, z, sin, z], -1))

    wb = w_in.astype(BF16)
    cq = mm_rownorm(h, wb, q_lat_gain[None, :].astype(F32), name + "_cq")
    ckv = mm_rownorm(h, wb, kv_lat_gain[None, :].astype(F32), name + "_ckv", col_off=q_rank)
    kpe = mm_plain(h, _mla_rope_layout(w_in[:, q_rank + kv_rank:]).astype(BF16), F32,
                   name + "_kpe", tn=V7X_LANES)

    def pad_head(t):
        return jnp.concatenate([t[..., :dn], _mla_rope_layout(t[..., dn:])], axis=-1)

    wq = pad_head(w_q_up.reshape(q_rank, nh, dn + dr)).reshape(q_rank, nh * 2 * V7X_LANES)
    t = x.shape[0]
    tm = _tile(t, MM_TM)
    rs = _rope_specs(t, seq, tm)
    qshape, qspec = _out_tile(t, wq.shape[1], BF16, tm, MM_TN)
    row256 = pl.BlockSpec((1, 2 * V7X_LANES), lambda i, j: (0, 0))
    row128 = pl.BlockSpec((1, V7X_LANES), lambda i, j: (0, 0))
    q = matmul(cq, [wq.astype(BF16)], [0], wq.shape[1], tm=tm, tn=MM_TN,
               epilogue=functools.partial(_ep_mla_q, n_valid=dn + dr),
               out_shapes=qshape, out_specs=qspec,
               aux=[pad_head(q_gain * ((dn + dr) ** -0.5 * LOG2E))[None, :].astype(F32),
                    tabs[0], tabs[1]],
               aux_specs=[row256, rs, rs], row_chunk=MM_ROW_CHUNK, name=name + "_q")

    n_kv = w_kv_up.shape[1]
    tn = _tile(n_kv, MM_TN)
    kshape, kspec = _out_tile(t, n_kv, BF16, tm, tn)
    vshape = jax.ShapeDtypeStruct((t, n_kv // 2), BF16)
    vspec = pl.BlockSpec((tm, tn // 2), lambda i, j: (i, j))
    k, v = matmul(ckv, [w_kv_up.astype(BF16)], [0], n_kv, tm=tm, tn=tn,
                  epilogue=functools.partial(_ep_mla_kv, n_valid=dn + dr),
                  out_shapes=(kshape, vshape), out_specs=(kspec, vspec),
                  aux=[kpe, k_gain[None, :dn].astype(F32),
                       _mla_rope_layout(k_gain[dn:])[None, :].astype(F32), tabs[0], tabs[1]],
                  aux_specs=[pl.BlockSpec((tm, V7X_LANES), lambda i, j: (i, 0)), row128, row128, rs, rs],
                  row_chunk=2 * MM_ROW_CHUNK, name=name + "_kv")
    o = flash_attention(q, k, v, batch=batch, seq=seq, heads=nh, hp=4, n_maps=1,
                        dk=2 * V7X_LANES, dv=dv, q_blk_off=0, k_blk_off=0, v_blk_off=0,
                        frame_causal=False, name=name + "_attn")
    return mm_residual(o, w_out.astype(BF16), x, 1.0, name + "_out")


def _mlstm(x, h, w_in, gate_bias, out_gain, w_out, batch, seq, name):
    d = x.shape[1]
    nh = MLSTM_HEADS
    dk, dv = d // (2 * nh), d // nh
    n_qkv = 2 * nh * dk + nh * dv
    wb = w_in.astype(BF16)
    qkv = mm_plain(h, wb, BF16, name + "_qkv", n_cols=n_qkv)
    o = mm_plain(h, wb, F32, name + "_o", n_cols=d, col_off=n_qkv)
    gates = mm_plain(h, _pad_lanes(w_in[:, n_qkv + d:]).astype(BF16), F32, name + "_g", tn=V7X_LANES)
    bias = _pad_lanes(jnp.concatenate([gate_bias[0], gate_bias[1]])[None, :])
    lane = jnp.arange(V7X_LANES)
    mult = ((lane >= nh) & (lane < 2 * nh)).astype(F32)[None, :]
    scanned = gate_scan(gates, bias, mult, batch, seq, name + "_scan")
    gate_cols, gate_rows = _gate_arrays(scanned, 2 * nh, batch, seq)
    y = mlstm_chunks(qkv, o, gate_cols, gate_rows, out_gain.astype(F32), batch=batch, seq=seq,
                     heads=nh, dk=dk, dv=dv, name=name + "_cell")
    return mm_residual(y, w_out.astype(BF16), x, 1.0, name + "_out")


def kernel(x, norm_ffn, ffn_w_in, ffn_w_out, norm_mix, fox_w_in, fox_q_gain, fox_k_gain, fox_f_bias, fox_w_out, diff_w_in, diff_q_gain, diff_k_gain, diff_lambda, diff_sub_gain, diff_w_out, mla_w_in, mla_q_lat_gain, mla_w_q_up, mla_kv_lat_gain, mla_w_kv_up, mla_q_gain, mla_k_gain, mla_w_out, mlstm_w_in, mlstm_gate_bias, mlstm_out_gain, mlstm_w_out):
    batch, seq, d = x.shape
    depth = norm_mix.shape[0]
    x = x.reshape(batch * seq, d)
    ffn_in = _interleave_gate_up(norm_ffn[..., None] * ffn_w_in).astype(BF16)
    ffn_out = ffn_w_out.astype(BF16)
    h = row_stats(x, "rows0")
    for i in range(depth):
        kind, occ = i % N_MIXERS, i // N_MIXERS
        gain = norm_mix[i][:, None]
        x, h = _ffn(x, h, ffn_in, ffn_out, (i, 0), f"l{i}_ffa")
        if kind == 0:
            x, h = _fox(x, h, gain * fox_w_in[occ], fox_q_gain[occ], fox_k_gain[occ],
                        fox_f_bias[occ], fox_w_out[occ], batch, seq, f"l{i}_fox")
        elif kind == 1:
            lam_init = 0.8 - 0.6 * math.exp(-0.3 * i)
            x, h = _diff(x, h, gain * diff_w_in[occ], diff_q_gain[occ], diff_k_gain[occ],
                         diff_lambda[occ], diff_sub_gain[occ], diff_w_out[occ], lam_init,
                         batch, seq, f"l{i}_diff")
        elif kind == 2:
            x, h = _mla(x, h, gain * mla_w_in[occ], mla_q_lat_gain[occ], mla_w_q_up[occ],
                        mla_kv_lat_gain[occ], mla_w_kv_up[occ], mla_q_gain[occ], mla_k_gain[occ],
                        mla_w_out[occ], batch, seq, f"l{i}_mla")
        else:
            x, h = _mlstm(x, h, gain * mlstm_w_in[occ], mlstm_gate_bias[occ], mlstm_out_gain[occ],
                          mlstm_w_out[occ], batch, seq, f"l{i}_mlstm")
        x, h = _ffn(x, h, ffn_in, ffn_out, (i, 1), f"l{i}_ffb")
    return x.reshape(batch, seq, d)
```

```python
import functools
import math
from typing import NamedTuple

import jax
import jax.numpy as jnp
from jax import lax
from jax.experimental import pallas as pl
from jax.experimental.pallas import tpu as pltpu

F32 = jnp.float32
BF16 = jnp.bfloat16

CHUNK = 64
EPS = 1e-6
ROPE_THETA = 10000.0
LOG2E = math.log2(math.e)
N_MIXERS = 4
FOX_HEADS = 32
DIFF_HEADS = 16
MLA_HEADS = 32
MLA_NOPE_DIM = 128
MLA_ROPE_DIM = 64
MLA_V_DIM = 128
MLSTM_HEADS = 8

V7X_LANES = 128
V7X_VMEM_LIMIT_BYTES = 56 * 1024 * 1024

MM_TM = 1024
MM_TN = 1024
MM_ROW_CHUNK = 256
MM_COL_CHUNK = 256
ATTN_TQ = 512
ATTN_TK = 512
SUM_ROWS = 16
MLSTM_LC = 256
NORM_TR = 256
SCAN_BLK = 512


def _tile(n, pref):
    t = min(n, pref)
    assert n % t == 0, (n, pref)
    return t


def _params(n_grid):
    return pltpu.CompilerParams(
        dimension_semantics=("arbitrary",) * n_grid,
        vmem_limit_bytes=V7X_VMEM_LIMIT_BYTES,
    )


class Rows(NamedTuple):
    xb: jax.Array
    ssq: jax.Array


def _row_stats_body(x_ref, xb_ref, ssq_ref):
    x = x_ref[...]
    xb_ref[...] = x.astype(xb_ref.dtype)
    sq = x * x
    part = sq[:, :V7X_LANES]
    for c in range(1, sq.shape[1] // V7X_LANES):
        part = part + sq[:, c * V7X_LANES:(c + 1) * V7X_LANES]
    ssq_ref[...] = part


def row_stats(x, name):
    t, d = x.shape
    tr = _tile(t, NORM_TR)
    xb, ssq = pl.pallas_call(
        _row_stats_body,
        out_shape=(jax.ShapeDtypeStruct((t, d), BF16), jax.ShapeDtypeStruct((t, V7X_LANES), F32)),
        grid=(t // tr,),
        in_specs=[pl.BlockSpec((tr, d), lambda i: (i, 0))],
        out_specs=(pl.BlockSpec((tr, d), lambda i: (i, 0)),
                   pl.BlockSpec((tr, V7X_LANES), lambda i: (i, 0))),
        compiler_params=_params(1),
        name=name,
    )(x)
    return Rows(xb, ssq)


def _mm_body(*refs, normed, n_w, n_aux, epilogue, row_chunk, col_chunk, n_acc_outs):
    a_ref = refs[0]
    refs = refs[1:]
    if normed:
        ssq_ref, refs = refs[0], refs[1:]
    w_refs = refs[:n_w]
    aux_refs = refs[n_w:n_w + n_aux]
    out_refs = refs[n_w + n_aux:]
    tm, tn = a_ref.shape[0], w_refs[0].shape[1]

    @pl.when(pl.program_id(1) == 0)
    def _():
        for o in out_refs[len(out_refs) - n_acc_outs:]:
            o[...] = jnp.zeros_like(o)

    for r0 in range(0, tm, row_chunk):
        rows = slice(r0, r0 + row_chunk)
        a = a_ref[rows, :]
        if normed:
            ssq = jnp.sum(ssq_ref[rows, :], axis=-1, keepdims=True)
            r = lax.rsqrt(ssq * (1.0 / a.shape[1]) + EPS)
        for c0 in range(0, tn, col_chunk):
            cols = slice(c0, c0 + col_chunk)

            def view(ref):
                ref = ref.at[rows] if ref.shape[0] == tm else ref
                return ref.at[:, cols] if ref.shape[1] == tn else ref

            accs = [jnp.dot(a, w[:, cols], preferred_element_type=F32) for w in w_refs]
            views = ([view(x) for x in aux_refs], [view(x) for x in out_refs])
            if normed and getattr(epilogue, "applies_row_scale", False):
                epilogue(accs, *views, r)
            else:
                if normed:
                    accs = [acc * r for acc in accs]
                epilogue(accs, *views)


def matmul(a, ws, w_blk_offsets, n_cols, *, tm, tn, epilogue, out_shapes, out_specs,
           aux=(), aux_specs=(), w_lead=(), row_chunk=None, col_chunk=None, n_acc_outs=0, name):
    normed = isinstance(a, Rows)
    a_arr = a.xb if normed else a
    t, k = a_arr.shape
    tm = _tile(t, tm)
    tn = _tile(n_cols, tn)
    in_specs = [pl.BlockSpec((tm, k), lambda i, j: (i, 0))]
    args = [a_arr]
    if normed:
        in_specs.append(pl.BlockSpec((tm, V7X_LANES), lambda i, j: (i, 0)))
        args.append(a.ssq)
    lead_blk = (None,) * len(w_lead)
    for off in w_blk_offsets:
        in_specs.append(pl.BlockSpec(lead_blk + (k, tn),
                                     lambda i, j, off=off: tuple(w_lead) + (0, j + off)))
    in_specs += list(aux_specs)
    body = functools.partial(_mm_body, normed=normed, n_w=len(ws), n_aux=len(aux),
                             epilogue=epilogue, row_chunk=_tile(tm, row_chunk or tm),
                             col_chunk=_tile(tn, col_chunk or tn), n_acc_outs=n_acc_outs)
    return pl.pallas_call(
        body,
        out_shape=out_shapes,
        grid=(t // tm, n_cols // tn),
        in_specs=in_specs,
        out_specs=out_specs,
        compiler_params=_params(2),
        name=name,
    )(*args, *ws, *aux)


def _ep_plain(accs, aux, outs):
    outs[0][...] = accs[0].astype(outs[0].dtype)


def _ep_swiglu(accs, aux, outs, r):
    g, u = accs
    y = (g * u) * (r * r) / (1.0 + jnp.exp2(g * (r * (-LOG2E))))
    outs[0][...] = y.astype(outs[0].dtype)


_ep_swiglu.applies_row_scale = True


def _ep_residual(accs, aux, outs, *, alpha):
    x_out, xb_out, ssq_out = outs
    x_new = aux[0][...] + alpha * accs[0]
    x_out[...] = x_new
    xb_out[...] = x_new.astype(xb_out.dtype)
    sq = x_new * x_new
    part = sq[:, :V7X_LANES]
    for c in range(1, sq.shape[1] // V7X_LANES):
        part = part + sq[:, c * V7X_LANES:(c + 1) * V7X_LANES]
    ssq_out[...] += part


def _rope(t, cosf, sinf):
    return t * cosf + pltpu.roll(t, V7X_LANES // 2, 1) * sinf


def _ep_headnorm(accs, aux, outs, *, rope):
    acc = accs[0]
    if rope:
        cosf, sinf = aux[1][...], aux[2][...]
    for c in range(acc.shape[1] // V7X_LANES):
        sl = slice(c * V7X_LANES, (c + 1) * V7X_LANES)
        t = acc[:, sl]
        y = t * lax.rsqrt(jnp.mean(t * t, axis=-1, keepdims=True) + EPS) * aux[0][:, sl]
        if rope:
            y = _rope(y, cosf, sinf)
        outs[0][:, sl] = y.astype(outs[0].dtype)


def _ep_rownorm(accs, aux, outs):
    t = accs[0]
    y = t * lax.rsqrt(jnp.mean(t * t, axis=-1, keepdims=True) + EPS) * aux[0][...]
    outs[0][...] = y.astype(outs[0].dtype)


def _ep_mla_q(accs, aux, outs, *, n_valid):
    acc = accs[0]
    gain, cosf, sinf = aux[0][...], aux[1][...], aux[2][...]
    w = 2 * V7X_LANES
    for c in range(acc.shape[1] // w):
        t = acc[:, c * w:(c + 1) * w]
        r = lax.rsqrt(jnp.sum(t * t, axis=-1, keepdims=True) / n_valid + EPS)
        y = t * r * gain
        o = c * w
        outs[0][:, o:o + V7X_LANES] = y[:, :V7X_LANES].astype(outs[0].dtype)
        outs[0][:, o + V7X_LANES:o + w] = _rope(y[:, V7X_LANES:], cosf, sinf).astype(outs[0].dtype)


def _ep_mla_kv(accs, aux, outs, *, n_valid):
    acc = accs[0]
    kpe, gain_nope, gain_pe, cosf, sinf = (r[...] for r in aux)
    k_out, v_out = outs
    ssq_pe = jnp.sum(kpe * kpe, axis=-1, keepdims=True)
    pe_rot = _rope(kpe * gain_pe, cosf, sinf)
    w = 2 * V7X_LANES
    for c in range(acc.shape[1] // w):
        kn = acc[:, c * w:c * w + V7X_LANES]
        v = acc[:, c * w + V7X_LANES:(c + 1) * w]
        r = lax.rsqrt((jnp.sum(kn * kn, axis=-1, keepdims=True) + ssq_pe) / n_valid + EPS)
        o = c * w
        k_out[:, o:o + V7X_LANES] = (kn * r * gain_nope).astype(k_out.dtype)
        k_out[:, o + V7X_LANES:o + w] = (pe_rot * r).astype(k_out.dtype)
        v_out[:, o // 2:o // 2 + V7X_LANES] = v.astype(v_out.dtype)


def _out_tile(t, n, dtype, tm, tn):
    tm, tn = _tile(t, tm), _tile(n, tn)
    return jax.ShapeDtypeStruct((t, n), dtype), pl.BlockSpec((tm, tn), lambda i, j: (i, j))


def _col_blk(col_off, n, tn):
    tn = _tile(n, tn)
    assert col_off % tn == 0, (col_off, tn)
    return col_off // tn


def _n_rows(a):
    return (a.xb if isinstance(a, Rows) else a).shape[0]


def mm_plain(a, w, out_dtype, name, n_cols=None, col_off=0, tm=MM_TM, tn=MM_TN):
    t, n = _n_rows(a), n_cols or w.shape[1]
    shape, spec = _out_tile(t, n, out_dtype, tm, tn)
    return matmul(a, [w], [_col_blk(col_off, n, tn)], n, tm=tm, tn=tn, epilogue=_ep_plain,
                  out_shapes=shape, out_specs=spec, name=name)


def mm_swiglu(a, w_in, name, w_lead=(), tm=MM_TM, tn=MM_TN // 2):
    t, n = _n_rows(a), w_in.shape[-1] // 2
    tn = _tile(n, tn)
    shape, spec = _out_tile(t, n, BF16, tm, tn)
    return matmul(a, [w_in, w_in], [0, n // tn], n, tm=tm, tn=tn, epilogue=_ep_swiglu,
                  out_shapes=shape, out_specs=spec, w_lead=w_lead, name=name)


def mm_residual(a, w, res, alpha, name, w_lead=(), tm=MM_TM, tn=MM_TN // 2):
    t, n = _n_rows(a), w.shape[-1]
    tm_ = _tile(t, tm)
    shape, spec = _out_tile(t, n, F32, tm, tn)
    x, xb, ssq = matmul(
        a, [w], [0], n, tm=tm, tn=tn, epilogue=functools.partial(_ep_residual, alpha=alpha),
        out_shapes=(shape, jax.ShapeDtypeStruct((t, n), BF16),
                    jax.ShapeDtypeStruct((t, V7X_LANES), F32)),
        out_specs=(spec, spec, pl.BlockSpec((tm_, V7X_LANES), lambda i, j: (i, 0))),
        aux=[res], aux_specs=[spec], w_lead=w_lead, col_chunk=MM_COL_CHUNK, n_acc_outs=1, name=name)
    return x, Rows(xb, ssq)


def _rope_specs(t, seq, tm):
    tm = _tile(t, tm)
    assert seq % tm == 0
    nblk = seq // tm
    return pl.BlockSpec((tm, V7X_LANES), lambda i, j: (i % nblk, 0))


def mm_headnorm(a, w, gain_row, name, rope_tabs=None, seq=None, tm=MM_TM, tn=MM_TN):
    t, n = _n_rows(a), gain_row.shape[1]
    tn_ = _tile(n, tn)
    shape, spec = _out_tile(t, n, BF16, tm, tn)
    aux = [gain_row]
    aux_specs = [pl.BlockSpec((1, tn_), lambda i, j: (0, j))]
    if rope_tabs is not None:
        rs = _rope_specs(t, seq, tm)
        aux += list(rope_tabs)
        aux_specs += [rs, rs]
    return matmul(a, [w], [0], n, tm=tm, tn=tn,
                  epilogue=functools.partial(_ep_headnorm, rope=rope_tabs is not None),
                  out_shapes=shape, out_specs=spec, aux=aux, aux_specs=aux_specs,
                  row_chunk=MM_ROW_CHUNK, name=name)


def mm_rownorm(a, w, gain_row, name, col_off=0, tm=MM_TM):
    t, n = _n_rows(a), gain_row.shape[1]
    shape, spec = _out_tile(t, n, BF16, tm, n)
    return matmul(a, [w], [_col_blk(col_off, n, n)], n, tm=tm, tn=n, epilogue=_ep_rownorm,
                  out_shapes=shape, out_specs=spec, aux=[gain_row],
                  aux_specs=[pl.BlockSpec((1, n), lambda i, j: (0, 0))], name=name)


def _gate_scan_body(x_ref, bias_ref, mult_ref, o_ref, carry_ref, *, blk):
    @pl.when(pl.program_id(1) == 0)
    def _():
        carry_ref[...] = jnp.zeros_like(carry_ref)

    y = x_ref[...] + bias_ref[...]
    c = jnp.minimum(y, 0.0) - jnp.log1p(jnp.exp(-jnp.abs(y)))
    row = lax.broadcasted_iota(jnp.int32, c.shape, 0)
    shift = 1
    while shift < blk:
        c = c + jnp.where(row >= shift, pltpu.roll(c, shift, 0), 0.0)
        shift *= 2
    c = c + carry_ref[...]
    carry_ref[...] = c[blk - 1:blk, :]
    mult = mult_ref[...]
    o_ref[...] = jnp.where(mult != 0.0, c * mult, y)


def gate_scan(x, bias_row, mult_row, batch, seq, name):
    blk = _tile(seq, SCAN_BLK)
    nblk = seq // blk
    spec = pl.BlockSpec((blk, V7X_LANES), lambda b, s: (b * nblk + s, 0))
    row = pl.BlockSpec((1, V7X_LANES), lambda b, s: (0, 0))
    return pl.pallas_call(
        functools.partial(_gate_scan_body, blk=blk),
        out_shape=jax.ShapeDtypeStruct(x.shape, F32),
        grid=(batch, nblk),
        in_specs=[spec, row, row],
        out_specs=spec,
        scratch_shapes=[pltpu.VMEM((1, V7X_LANES), F32)],
        compiler_params=_params(2),
        name=name,
    )(x, bias_row, mult_row)


def _pad_lanes(v, width=V7X_LANES):
    return jnp.pad(v, [(0, 0)] * (v.ndim - 1) + [(0, width - v.shape[-1])])


class _Stream:
    def __init__(self, q_cols, k_cols, head):
        self.q_cols, self.k_cols, self.head = q_cols, k_cols, head


def _flash_body(*refs, streams, hp, dv, tq, tk, seq, frame_causal, decay, n_extra, finalize):
    q_ref, k_ref, v_ref = refs[:3]
    pos = 3
    if decay:
        cq_ref, ck_ref = refs[3:5]
        pos = 5
    extra = refs[pos:pos + n_extra]
    o_ref = refs[pos + n_extra]
    vt_ref = refs[pos + n_extra + 1]
    ckt_ref = refs[pos + n_extra + 2] if decay else None
    t_ref, acc_ref, m_ref = refs[-3:]
    qi = pl.program_id(2)

    @pl.when(qi == 0)
    def _():
        def fill(c, _):
            start = pl.multiple_of(c * tq, tq)
            ones_rows = (lax.broadcasted_iota(jnp.int32, (SUM_ROWS, tq), 0) == 0).astype(vt_ref.dtype)
            for e in range(hp):
                vt_ref[e, :dv, pl.ds(start, tq)] = v_ref[pl.ds(start, tq), e * dv:(e + 1) * dv].T
                vt_ref[e, dv:, pl.ds(start, tq)] = ones_rows
                if decay:
                    ck = jnp.broadcast_to(ck_ref[e, :, pl.ds(start, tq)], (V7X_LANES, tq))
                    ckt_ref[e, pl.ds(start, tq), :] = ck.T
            return 0
        lax.fori_loop(0, seq // tq, fill, 0)

    def diag_mask(d):
        key = lax.broadcasted_iota(jnp.int32, (tk, tq), 0) + d * tk
        qry = lax.broadcasted_iota(jnp.int32, (tk, tq), 1)
        if frame_causal:
            return key <= qry
        shift = CHUNK.bit_length() - 1
        return jnp.right_shift(key, shift) <= jnp.right_shift(qry, shift)

    n = len(streams)

    def scores(j, slot, mask):
        start = pl.multiple_of(j * tk, tk)
        base = slot * n
        for si, st in enumerate(streams):
            t = lax.dot_general(k_ref[pl.ds(start, tk), st.k_cols], q_ref[:, st.q_cols],
                                (((1,), (1,)), ((), ())), preferred_element_type=F32)
            if decay:
                ckb = ckt_ref[st.head, pl.ds(start, tk), :]
                t = t - jnp.concatenate([ckb] * (tq // V7X_LANES), axis=1)
            if mask is not None:
                t = jnp.where(mask, t, -jnp.inf)
            t_ref[base + si] = t

    def update(j, slot):
        start = pl.multiple_of(j * tk, tk)
        base = slot * n
        for si, st in enumerate(streams):
            vt = vt_ref[st.head, :, pl.ds(start, tk)]
            m = m_ref[si]
            cmax = jnp.max(t_ref[base + si], axis=0, keepdims=True)
            if decay:
                cq = cq_ref[st.head]
                m_new = jnp.maximum(m, cmax + cq)
                mt = m_new - cq
            else:
                m_new = jnp.maximum(m, cmax)
                mt = m_new
            alpha = jnp.exp2(m - m_new)
            p = jnp.exp2(t_ref[base + si] - mt)
            acc_ref[si] = alpha * acc_ref[si] + jnp.dot(vt, p.astype(BF16),
                                                        preferred_element_type=F32)
            m_ref[si] = m_new

    assert tq == tk
    acc_ref[...] = jnp.zeros_like(acc_ref)
    m_ref[...] = jnp.full(m_ref.shape, -jnp.inf, F32)

    mask = diag_mask(0)
    n_pairs = jnp.maximum(qi - 1, 0) // 2
    tail = 2 * n_pairs

    @pl.when(qi == 0)
    def _():
        scores(0, 0, mask)

    @pl.when(qi > 0)
    def _():
        scores(0, 0, None)

    def pair(i, _):
        j = 2 * i
        scores(j + 1, 1, None)
        update(j, 0)
        scores(j + 2, 0, None)
        update(j + 1, 1)
        return 0

    lax.fori_loop(0, n_pairs, pair, 0)

    @pl.when(qi == 0)
    def _():
        update(0, 0)

    @pl.when(qi - tail == 1)
    def _():
        scores(qi, 1, mask)
        update(tail, 0)
        update(qi, 1)

    @pl.when(qi - tail == 2)
    def _():
        scores(tail + 1, 1, None)
        update(tail, 0)
        scores(qi, 0, mask)
        update(tail + 1, 1)
        update(qi, 0)

    finalize([acc_ref[si, :dv] / acc_ref[si, dv:dv + 1] for si in range(n)], extra, o_ref)


def _fin_heads(outs, extra, o_ref):
    dv = outs[0].shape[0]
    for e, o in enumerate(outs):
        o_ref[:, e * dv:(e + 1) * dv] = o.T.astype(o_ref.dtype)


def _fin_diff(outs, extra, o_ref, *, lam_init):
    lp = extra[0][...]
    sub_gain = extra[1][...]
    lam = (jnp.exp(jnp.sum(lp[0:1] * lp[1:2], axis=-1, keepdims=True))
           - jnp.exp(jnp.sum(lp[2:3] * lp[3:4], axis=-1, keepdims=True)) + lam_init)
    dv = outs[0].shape[0]
    for e in range(len(outs) // 2):
        y = (outs[2 * e] - lam * outs[2 * e + 1]).T
        y = y * lax.rsqrt(jnp.mean(y * y, axis=-1, keepdims=True) + EPS) * sub_gain
        o_ref[:, e * dv:(e + 1) * dv] = (y * (1.0 - lam_init)).astype(o_ref.dtype)


def flash_attention(q_arr, k_arr, v_arr, *, batch, seq, heads, hp, n_maps, dk, dv,
                    q_blk_off, k_blk_off, v_blk_off, frame_causal,
                    decay_rows=None, extra=(), extra_specs=(), finalize=_fin_heads, name):
    tq = _tile(seq, ATTN_TQ)
    tk = _tile(tq, ATTN_TK)
    assert tk % CHUNK == 0 and tk % V7X_LANES == 0 and heads % hp == 0
    nq = seq // tq
    qw = hp * n_maps * dk
    vw = hp * dv
    decay = decay_rows is not None
    streams = []
    for e in range(hp):
        for g in range(n_maps):
            c = (e * n_maps + g) * dk
            streams.append(_Stream(slice(c, c + dk), slice(c, c + dk), e))
    in_specs = [
        pl.BlockSpec((tq, qw), lambda b, h, i: (b * nq + i, h + q_blk_off)),
        pl.BlockSpec((seq, qw), lambda b, h, i: (b, h + k_blk_off)),
        pl.BlockSpec((seq, vw), lambda b, h, i: (b, h + v_blk_off)),
    ]
    args = [q_arr, k_arr, v_arr]
    scratch = [pltpu.VMEM((hp, dv + SUM_ROWS, seq), BF16)]
    if decay:
        in_specs += [pl.BlockSpec((None, hp, 1, tq), lambda b, h, i: (b, h, 0, i)),
                     pl.BlockSpec((None, hp, 1, seq), lambda b, h, i: (b, h, 0, 0))]
        args += [decay_rows, decay_rows]
        scratch.append(pltpu.VMEM((hp, seq, V7X_LANES), F32))
    in_specs += list(extra_specs)
    args += list(extra)
    scratch.append(pltpu.VMEM((2 * len(streams), tk, tq), F32))
    scratch.append(pltpu.VMEM((len(streams), dv + SUM_ROWS, tq), F32))
    scratch.append(pltpu.VMEM((len(streams), 1, tq), F32))
    body = functools.partial(_flash_body, streams=streams, hp=hp, dv=dv, tq=tq, tk=tk, seq=seq,
                             frame_causal=frame_causal, decay=decay,
                             n_extra=len(extra), finalize=finalize)
    return pl.pallas_call(
        body,
        out_shape=jax.ShapeDtypeStruct((batch * seq, heads * dv), BF16),
        grid=(batch, heads // hp, nq),
        in_specs=in_specs,
        out_specs=pl.BlockSpec((tq, vw), lambda b, h, i: (b * nq + i, h)),
        scratch_shapes=scratch,
        compiler_params=_params(3),
        name=name,
    )(*args)


def _mlstm_body(q_ref, k_ref, v_ref, o_ref, ipc_ref, cumc_ref, gr_ref, gain_ref, y_ref,
                c_ref, n_ref, m_ref, cprev_ref, *, lc, heads, dk):
    h = pl.program_id(1)

    @pl.when(pl.program_id(2) == 0)
    def _():
        c_ref[...] = jnp.zeros_like(c_ref)
        n_ref[...] = jnp.zeros_like(n_ref)
        m_ref[...] = jnp.zeros_like(m_ref)
        cprev_ref[...] = jnp.zeros_like(cprev_ref)

    qscale = dk ** -0.5
    q = q_ref[...]
    k = k_ref[...]
    v = v_ref[...]
    ip_col = ipc_ref[...]
    ip_row = gr_ref[pl.ds(h, 1), :]
    cum_row = gr_ref[pl.ds(heads + h, 1), :]
    cprev = cprev_ref[...]
    m_prev = m_ref[...]
    b_col = cumc_ref[...] - cprev
    b_row = cum_row - cprev
    b_last = b_row[:, lc - 1:lc]

    row = lax.broadcasted_iota(jnp.int32, (lc, lc), 0)
    col = lax.broadcasted_iota(jnp.int32, (lc, lc), 1)
    d = jnp.where(col <= row, b_col - b_row + ip_row, -jnp.inf)
    inter = b_col + m_prev
    m_row = jnp.maximum(jnp.max(d, axis=-1, keepdims=True), inter)
    w_intra = jnp.exp(d - m_row)
    w_inter = jnp.exp(inter - m_row)

    qk = lax.dot_general(q, k, (((1,), (1,)), ((), ())), preferred_element_type=F32) * qscale
    a = w_intra * qk
    c_old = c_ref[...]
    num = (jnp.dot(a.astype(BF16), v, preferred_element_type=F32)
           + w_inter * (jnp.dot(q, c_old.astype(BF16), preferred_element_type=F32) * qscale))
    qn = jnp.sum(q.astype(F32) * n_ref[...], axis=-1, keepdims=True) * qscale
    den = jnp.sum(a, axis=-1, keepdims=True) + w_inter * qn
    hc = num / jnp.maximum(jnp.abs(den), jnp.exp(-m_row))

    g_row = b_last - b_row + ip_row
    g_col = b_last - b_col + ip_col
    m_new = jnp.maximum(b_last + m_prev, jnp.max(g_row, axis=-1, keepdims=True))
    decay = jnp.exp(b_last + m_prev - m_new)
    kw = k.astype(F32) * jnp.exp(g_col - m_new)
    c_ref[...] = decay * c_old + jnp.dot(kw.T.astype(BF16), v, preferred_element_type=F32)
    n_ref[...] = decay * n_ref[...] + jnp.sum(kw, axis=0, keepdims=True)
    m_ref[...] = m_new
    cprev_ref[...] = cum_row[:, lc - 1:lc]

    y = hc * lax.rsqrt(jnp.mean(hc * hc, axis=-1, keepdims=True) + EPS) * gain_ref[...]
    y_ref[...] = (y * jax.nn.sigmoid(o_ref[...])).astype(y_ref.dtype)


def mlstm_chunks(qkv, o, gate_cols, gate_rows, out_gain, *, batch, seq, heads, dk, dv, name):
    lc = _tile(seq, MLSTM_LC)
    nc = seq // lc
    kb, vb = heads, (2 * heads * dk) // dv
    col_spec = lambda off: pl.BlockSpec((None, None, lc, 1), lambda b, h, c: (b, h + off, c, 0))
    in_specs = [
        pl.BlockSpec((lc, dk), lambda b, h, c: (b * nc + c, h)),
        pl.BlockSpec((lc, dk), lambda b, h, c: (b * nc + c, kb + h)),
        pl.BlockSpec((lc, dv), lambda b, h, c: (b * nc + c, vb + h)),
        pl.BlockSpec((lc, dv), lambda b, h, c: (b * nc + c, h)),
        col_spec(0),
        col_spec(heads),
        pl.BlockSpec((None, 2 * heads, lc), lambda b, h, c: (b, 0, c)),
        pl.BlockSpec((None, 1, dv), lambda b, h, c: (h, 0, 0)),
    ]
    return pl.pallas_call(
        functools.partial(_mlstm_body, lc=lc, heads=heads, dk=dk),
        out_shape=jax.ShapeDtypeStruct((batch * seq, heads * dv), BF16),
        grid=(batch, heads, nc),
        in_specs=in_specs,
        out_specs=pl.BlockSpec((lc, dv), lambda b, h, c: (b * nc + c, h)),
        scratch_shapes=[pltpu.VMEM((dk, dv), F32), pltpu.VMEM((1, dk), F32),
                        pltpu.VMEM((1, 1), F32), pltpu.VMEM((1, 1), F32)],
        compiler_params=_params(3),
        name=name,
    )(qkv, qkv, qkv, o, gate_cols, gate_cols, gate_rows, out_gain.reshape(heads, 1, dv))


def _rope_tables(seq, dim):
    pos = jnp.arange(seq, dtype=F32)
    inv = ROPE_THETA ** (-jnp.arange(0, dim, 2, dtype=F32) / dim)
    ang = pos[:, None] * inv[None, :]
    return jnp.cos(ang), jnp.sin(ang)


def _ffn(x, h, w_in_all, w_out_all, lead, name):
    act = mm_swiglu(h, w_in_all, name + "_in", w_lead=lead)
    return mm_residual(act, w_out_all, x, 0.5, name + "_out", w_lead=lead)


def _gate_arrays(gates, n_rows, batch, seq):
    rows = gates.reshape(batch, seq, V7X_LANES)[:, :, :n_rows].transpose(0, 2, 1)
    return rows[..., None], rows


def _fox(x, h, w_in, q_gain, k_gain, f_bias, w_out, batch, seq, name):
    d = x.shape[1]
    hd = d // FOX_HEADS
    hp = 4
    q_scale = hd ** -0.5 * LOG2E
    gain_row = jnp.concatenate([jnp.tile(q_gain * q_scale, FOX_HEADS),
                                jnp.tile(k_gain, FOX_HEADS)])[None, :]
    wb = w_in.astype(BF16)
    qk = mm_headnorm(h, wb, gain_row.astype(F32), name + "_qk")
    v = mm_plain(h, wb, BF16, name + "_v", n_cols=d, col_off=2 * d)
    f = mm_plain(h, _pad_lanes(w_in[:, 3 * d:]).astype(BF16), F32, name + "_f", tn=V7X_LANES)
    mult = jnp.where(jnp.arange(V7X_LANES) < FOX_HEADS, LOG2E, 0.0).astype(F32)[None, :]
    cum = gate_scan(f, _pad_lanes(f_bias[None, :]), mult, batch, seq, name + "_scan")
    _, cum_rows = _gate_arrays(cum, FOX_HEADS, batch, seq)
    o = flash_attention(qk, qk, v, batch=batch, seq=seq, heads=FOX_HEADS, hp=hp, n_maps=1,
                        dk=hd, dv=hd, q_blk_off=0, k_blk_off=FOX_HEADS // hp, v_blk_off=0,
                        frame_causal=True, decay_rows=cum_rows[:, :, None, :],
                        name=name + "_attn")
    return mm_residual(o, w_out.astype(BF16), x, 1.0, name + "_out")


def _diff(x, h, w_in, q_gain, k_gain, lam_p, sub_gain, w_out, lam_init, batch, seq, name):
    d = x.shape[1]
    hd = d // (2 * DIFF_HEADS)
    dv = 2 * hd
    n_maps = 2 * DIFF_HEADS
    cos, sin = _rope_tables(seq, hd)
    tabs = (jnp.concatenate([cos, cos], -1), jnp.concatenate([-sin, sin], -1))
    q_scale = hd ** -0.5 * LOG2E
    gain_row = jnp.concatenate([jnp.tile(q_gain * q_scale, n_maps), jnp.tile(k_gain, n_maps)])[None, :]
    wb = w_in.astype(BF16)
    qk = mm_headnorm(h, wb, gain_row.astype(F32), name + "_qk", rope_tabs=tabs, seq=seq)
    v = mm_plain(h, wb, BF16, name + "_v", n_cols=DIFF_HEADS * dv, col_off=2 * d)
    const = lambda shape: pl.BlockSpec(shape, lambda b, hh, i: (0,) * len(shape))
    hp = 2
    o = flash_attention(qk, qk, v, batch=batch, seq=seq, heads=DIFF_HEADS, hp=hp, n_maps=2,
                        dk=hd, dv=dv, q_blk_off=0, k_blk_off=DIFF_HEADS // hp, v_blk_off=0,
                        frame_causal=False, extra=[lam_p.astype(F32), sub_gain[None, :].astype(F32)],
                        extra_specs=[const((4, hd)), const((1, dv))],
                        finalize=functools.partial(_fin_diff, lam_init=lam_init),
                        name=name + "_attn")
    return mm_residual(o, w_out.astype(BF16), x, 1.0, name + "_out")


def _mla_rope_layout(t):
    half = MLA_ROPE_DIM // 2
    z = jnp.zeros(t.shape[:-1] + (V7X_LANES // 2 - half,), t.dtype)
    return jnp.concatenate([t[..., :half], z, t[..., half:], z], axis=-1)


def _mla(x, h, w_in, q_lat_gain, w_q_up, kv_lat_gain, w_kv_up, q_gain, k_gain, w_out,
         batch, seq, name):
    nh, dn, dr, dv = MLA_HEADS, MLA_NOPE_DIM, MLA_ROPE_DIM, MLA_V_DIM
    q_rank, kv_rank = w_q_up.shape[0], w_kv_up.shape[0]
    cos, sin = _rope_tables(seq, dr)
    z = jnp.zeros_like(cos)
    tabs = (jnp.concatenate([cos, z, cos, z], -1), jnp.concatenate([-sin, z, sin, z], -1))

    wb = w_in.astype(BF16)
    cq = mm_rownorm(h, wb, q_lat_gain[None, :].astype(F32), name + "_cq")
    ckv = mm_rownorm(h, wb, kv_lat_gain[None, :].astype(F32), name + "_ckv", col_off=q_rank)
    kpe = mm_plain(h, _mla_rope_layout(w_in[:, q_rank + kv_rank:]).astype(BF16), F32,
                   name + "_kpe", tn=V7X_LANES)

    def pad_head(t):
        return jnp.concatenate([t[..., :dn], _mla_rope_layout(t[..., dn:])], axis=-1)

    wq = pad_head(w_q_up.reshape(q_rank, nh, dn + dr)).reshape(q_rank, nh * 2 * V7X_LANES)
    t = x.shape[0]
    tm = _tile(t, MM_TM)
    rs = _rope_specs(t, seq, tm)
    qshape, qspec = _out_tile(t, wq.shape[1], BF16, tm, MM_TN)
    row256 = pl.BlockSpec((1, 2 * V7X_LANES), lambda i, j: (0, 0))
    row128 = pl.BlockSpec((1, V7X_LANES), lambda i, j: (0, 0))
    q = matmul(cq, [wq.astype(BF16)], [0], wq.shape[1], tm=tm, tn=MM_TN,
               epilogue=functools.partial(_ep_mla_q, n_valid=dn + dr),
               out_shapes=qshape, out_specs=qspec,
               aux=[pad_head(q_gain * ((dn + dr) ** -0.5 * LOG2E))[None, :].astype(F32),
                    tabs[0], tabs[1]],
               aux_specs=[row256, rs, rs], row_chunk=MM_ROW_CHUNK, name=name + "_q")

    n_kv = w_kv_up.shape[1]
    tn = _tile(n_kv, MM_TN)
    kshape, kspec = _out_tile(t, n_kv, BF16, tm, tn)
    vshape = jax.ShapeDtypeStruct((t, n_kv // 2), BF16)
    vspec = pl.BlockSpec((tm, tn // 2), lambda i, j: (i, j))
    k, v = matmul(ckv, [w_kv_up.astype(BF16)], [0], n_kv, tm=tm, tn=tn,
                  epilogue=functools.partial(_ep_mla_kv, n_valid=dn + dr),
                  out_shapes=(kshape, vshape), out_specs=(kspec, vspec),
                  aux=[kpe, k_gain[None, :dn].astype(F32),
                       _mla_rope_layout(k_gain[dn:])[None, :].astype(F32), tabs[0], tabs[1]],
                  aux_specs=[pl.BlockSpec((tm, V7X_LANES), lambda i, j: (i, 0)), row128, row128, rs, rs],
                  row_chunk=2 * MM_ROW_CHUNK, name=name + "_kv")
    o = flash_attention(q, k, v, batch=batch, seq=seq, heads=nh, hp=4, n_maps=1,
                        dk=2 * V7X_LANES, dv=dv, q_blk_off=0, k_blk_off=0, v_blk_off=0,
                        frame_causal=False, name=name + "_attn")
    return mm_residual(o, w_out.astype(BF16), x, 1.0, name + "_out")


def _mlstm(x, h, w_in, gate_bias, out_gain, w_out, batch, seq, name):
    d = x.shape[1]
    nh = MLSTM_HEADS
    dk, dv = d // (2 * nh), d // nh
    n_qkv = 2 * nh * dk + nh * dv
    wb = w_in.astype(BF16)
    qkv = mm_plain(h, wb, BF16, name + "_qkv", n_cols=n_qkv)
    o = mm_plain(h, wb, F32, name + "_o", n_cols=d, col_off=n_qkv)
    gates = mm_plain(h, _pad_lanes(w_in[:, n_qkv + d:]).astype(BF16), F32, name + "_g", tn=V7X_LANES)
    bias = _pad_lanes(jnp.concatenate([gate_bias[0], gate_bias[1]])[None, :])
    lane = jnp.arange(V7X_LANES)
    mult = ((lane >= nh) & (lane < 2 * nh)).astype(F32)[None, :]
    scanned = gate_scan(gates, bias, mult, batch, seq, name + "_scan")
    gate_cols, gate_rows = _gate_arrays(scanned, 2 * nh, batch, seq)
    y = mlstm_chunks(qkv, o, gate_cols, gate_rows, out_gain.astype(F32), batch=batch, seq=seq,
                     heads=nh, dk=dk, dv=dv, name=name + "_cell")
    return mm_residual(y, w_out.astype(BF16), x, 1.0, name + "_out")


def kernel(x, norm_ffn, ffn_w_in, ffn_w_out, norm_mix, fox_w_in, fox_q_gain, fox_k_gain, fox_f_bias, fox_w_out, diff_w_in, diff_q_gain, diff_k_gain, diff_lambda, diff_sub_gain, diff_w_out, mla_w_in, mla_q_lat_gain, mla_w_q_up, mla_kv_lat_gain, mla_w_kv_up, mla_q_gain, mla_k_gain, mla_w_out, mlstm_w_in, mlstm_gate_bias, mlstm_out_gain, mlstm_w_out):
    batch, seq, d = x.shape
    depth = norm_mix.shape[0]
    x = x.reshape(batch * seq, d)
    ffn_in = (norm_ffn[..., None] * ffn_w_in).astype(BF16)
    ffn_out = ffn_w_out.astype(BF16)
    h = row_stats(x, "rows0")
    for i in range(depth):
        kind, occ = i % N_MIXERS, i // N_MIXERS
        gain = norm_mix[i][:, None]
        x, h = _ffn(x, h, ffn_in, ffn_out, (i, 0), f"l{i}_ffa")
        if kind == 0:
            x, h = _fox(x, h, gain * fox_w_in[occ], fox_q_gain[occ], fox_k_gain[occ],
                        fox_f_bias[occ], fox_w_out[occ], batch, seq, f"l{i}_fox")
        elif kind == 1:
            lam_init = 0.8 - 0.6 * math.exp(-0.3 * i)
            x, h = _diff(x, h, gain * diff_w_in[occ], diff_q_gain[occ], diff_k_gain[occ],
                         diff_lambda[occ], diff_sub_gain[occ], diff_w_out[occ], lam_init,
                         batch, seq, f"l{i}_diff")
        elif kind == 2:
            x, h = _mla(x, h, gain * mla_w_in[occ], mla_q_lat_gain[occ], mla_w_q_up[occ],
                        mla_kv_lat_gain[occ], mla_w_kv_up[occ], mla_q_gain[occ], mla_k_gain[occ],
                        mla_w_out[occ], batch, seq, f"l{i}_mla")
        else:
            x, h = _mlstm(x, h, gain * mlstm_w_in[occ], mlstm_gate_bias[occ], mlstm_out_gain[occ],
                          mlstm_w_out[occ], batch, seq, f"l{i}_mlstm")
        x, h = _ffn(x, h, ffn_in, ffn_out, (i, 1), f"l{i}_ffb")
    return x.reshape(batch, seq, d)
```
